```python
import math
import jax, jax.numpy as jnp
from jax import lax
import numpy as np

D_MODEL = 1024
BATCH = 4
SEQ = 4096
DEPTH = 4

CHUNK = 64
N_A_LAYERS = DEPTH // 2
N_B_LAYERS = DEPTH - N_A_LAYERS
EXPAND = 2
D_INNER = EXPAND * D_MODEL
GMLP_BLOCK = 128
GMLP_GROUPS = 8
GMLP_GROUP_DIM = D_INNER // GMLP_GROUPS
SB_HEADS = 16
SB_HEAD_DIM = D_INNER // SB_HEADS
Q_BLOCK = 128
EPS = 1e-6

kernel_name = 'yoco_gmlp_stickbreaking_encoder'


def rmsnorm(x, g):
    xf = x.astype(jnp.float32)
    y = xf * lax.rsqrt(jnp.mean(xf * xf, axis=-1, keepdims=True) + EPS)
    return (y * g.astype(jnp.float32)).astype(x.dtype)


def layernorm(x, g, b):
    xf = x.astype(jnp.float32)
    mu = jnp.mean(xf, axis=-1, keepdims=True)
    xc = xf - mu
    var = jnp.mean(xc * xc, axis=-1, keepdims=True)
    y = xc * lax.rsqrt(var + EPS) * g.astype(jnp.float32) + b.astype(jnp.float32)
    return y.astype(x.dtype)


def chunk_causal_mask(n):
    c = jnp.arange(n) // CHUNK
    return c[:, None] >= c[None, :]


def gmlp_mixer(h, w_in, ln_g, ln_b, w_s, b_s, w_out):
    B, S, _ = h.shape
    u, v, z = jnp.split(h @ w_in, 3, axis=-1)
    u = jax.nn.gelu(u)
    v = layernorm(jax.nn.gelu(v), ln_g, ln_b)
    nb = S // GMLP_BLOCK
    v = v.reshape(B, nb, GMLP_BLOCK, GMLP_GROUPS, GMLP_GROUP_DIM)
    w = w_s * chunk_causal_mask(GMLP_BLOCK)[None].astype(w_s.dtype)
    s = jnp.einsum('gts,bnsgc->bntgc', w, v) + b_s.T[None, None, :, :, None]
    s = s.reshape(B, S, D_INNER)
    y = u * s * jax.nn.silu(z)
    return y @ w_out


def stick_breaking_mixer(h, k, v, w_in, w_out):
    B, S, _ = h.shape
    q, z = jnp.split(h @ w_in, 2, axis=-1)
    q = q.reshape(B, S, SB_HEADS, SB_HEAD_DIM).transpose(0, 2, 1, 3)
    scale = 1.0 / math.sqrt(SB_HEAD_DIM)
    outs = []
    for start in range(0, S, Q_BLOCK):
        end = start + Q_BLOCK
        qb = q[:, :, start:end].astype(jnp.float32)
        kb = k[:, :, :end].astype(jnp.float32)
        vb = v[:, :, :end].astype(jnp.float32)
        logits = jnp.einsum('bhtd,bhsd->bhts', qb, kb) * scale
        strict = jnp.arange(end)[None, :] < jnp.arange(start, end)[:, None]
        log_beta = jax.nn.log_sigmoid(logits)
        log_1m = jnp.where(strict, jax.nn.log_sigmoid(-logits), 0.0)
        after = lax.cumsum(log_1m, axis=3, reverse=True) - log_1m
        att = jnp.where(strict, jnp.exp(log_beta + after), 0.0)
        outs.append(jnp.einsum('bhts,bhsd->bhtd', att, vb))
    o = jnp.concatenate(outs, axis=2).astype(h.dtype)
    o = o.transpose(0, 2, 1, 3).reshape(B, S, D_INNER)
    y = o * jax.nn.silu(z)
    return y @ w_out


def setup_inputs(seed: int = 0) -> dict:
    key = jax.random.key(seed)
    ks = jax.random.split(key, 16)

    def dense(k, shape, fan_in, mult=1.0):
        return jax.random.normal(k, shape, jnp.float32) * (mult * fan_in ** -0.5)

    def gain(k, shape):
        return 1.0 + 0.05 * jax.random.normal(k, shape, jnp.float32)

    x = jax.random.normal(ks[0], (BATCH, SEQ, D_MODEL), jnp.float32)
    a_norm = gain(ks[1], (N_A_LAYERS, D_MODEL))
    a_w_in = dense(ks[2], (N_A_LAYERS, D_MODEL, 3 * D_INNER), D_MODEL)
    a_ln_g = gain(ks[3], (N_A_LAYERS, D_INNER))
    a_ln_b = 0.02 * jax.random.normal(ks[4], (N_A_LAYERS, D_INNER), jnp.float32)
    a_w_s = dense(ks[5], (N_A_LAYERS, GMLP_GROUPS, GMLP_BLOCK, GMLP_BLOCK), GMLP_BLOCK, 0.5)
    a_b_s = 1.0 + 0.1 * jax.random.normal(ks[6], (N_A_LAYERS, GMLP_GROUPS, GMLP_BLOCK), jnp.float32)
    a_w_out = dense(ks[7], (N_A_LAYERS, D_INNER, D_MODEL), D_INNER)
    kv_norm = gain(ks[8], (D_MODEL,))
    w_kv = dense(ks[9], (D_MODEL, 2 * D_INNER), D_MODEL)
    b_norm = gain(ks[10], (N_B_LAYERS, D_MODEL))
    b_w_in = dense(ks[11], (N_B_LAYERS, D_MODEL, 2 * D_INNER), D_MODEL)
    b_w_out = dense(ks[12], (N_B_LAYERS, D_INNER, D_MODEL), D_INNER)
    final_norm = gain(ks[13], (D_MODEL,))
    return {'x': x, 'a_norm': a_norm, 'a_w_in': a_w_in, 'a_ln_g': a_ln_g, 'a_ln_b': a_ln_b,
            'a_w_s': a_w_s, 'a_b_s': a_b_s, 'a_w_out': a_w_out, 'kv_norm': kv_norm, 'w_kv': w_kv,
            'b_norm': b_norm, 'b_w_in': b_w_in, 'b_w_out': b_w_out, 'final_norm': final_norm}


def reference(x, a_norm, a_w_in, a_ln_g, a_ln_b, a_w_s, a_b_s, a_w_out, kv_norm, w_kv,
              b_norm, b_w_in, b_w_out, final_norm):
    B, S, _ = x.shape
    k = v = None
    for layer in range(DEPTH):
        if layer < N_A_LAYERS:
            i = layer
            h = rmsnorm(x, a_norm[i])
            x = x + gmlp_mixer(h, a_w_in[i], a_ln_g[i], a_ln_b[i], a_w_s[i], a_b_s[i], a_w_out[i])
        else:
            if k is None:
                hk = rmsnorm(x, kv_norm)
                k, v = jnp.split(hk @ w_kv, 2, axis=-1)
                k = k.reshape(B, S, SB_HEADS, SB_HEAD_DIM).transpose(0, 2, 1, 3)
                v = v.reshape(B, S, SB_HEADS, SB_HEAD_DIM).transpose(0, 2, 1, 3)
            i = layer - N_A_LAYERS
            h = rmsnorm(x, b_norm[i])
            x = x + stick_breaking_mixer(h, k, v, b_w_in[i], b_w_out[i])
    return rmsnorm(x, final_norm)
```

```python
import functools
import math

import jax
import jax.numpy as jnp
from jax import lax
from jax.experimental import pallas as pl
from jax.experimental.pallas import tpu as pltpu

EPS = 1e-6
CHUNK = 64
GMLP_BLOCK = 128
GMLP_GROUPS = 8
SB_HEADS = 16

VMEM_LIMIT_BYTES_V7X = 56 * 1024 * 1024

GMLP_TILE = 256
PROJ_TILE = 512
ATT_TQ = 128
ATT_TK = 256

ATT_DEAD_LOG = -105.0
MASKED_LOGIT = -1e30

_BF16 = jnp.bfloat16
_F32 = jnp.float32


def _dot(a, b):
    return jnp.dot(a, b, preferred_element_type=_F32)


def _gelu(x):
    c = math.sqrt(2.0 / math.pi)
    return x * (0.5 * (1.0 + jnp.tanh(c * (x + 0.044715 * (x * x * x)))))


def _silu(z):
    return z * (1.0 / (1.0 + jnp.exp(-z)))


def _rmsnorm(x, g):
    return x * lax.rsqrt(jnp.mean(x * x, axis=-1, keepdims=True) + EPS) * g


def _const_spec(shape):
    nd = len(shape)
    return pl.BlockSpec(shape, lambda *_: (0,) * nd, pipeline_mode=pl.Buffered(1))


def _params(n_axes):
    return pltpu.CompilerParams(
        dimension_semantics=("arbitrary",) * n_axes,
        vmem_limit_bytes=VMEM_LIMIT_BYTES_V7X)


def _gmlp_kernel(x_ref, g_ref, win_ref, lng_ref, lnb_ref, ws_ref, bs_ref, wout_ref,
                 o_ref, vn_ref, y_ref, *, e, groups, blk):
    tm = x_ref.shape[0]
    gd = e // groups
    x = x_ref[...]
    h = _rmsnorm(x, g_ref[...]).astype(_BF16)

    v = _gelu(_dot(h, win_ref[:, e:2 * e]))
    mu = jnp.mean(v, axis=-1, keepdims=True)
    vc = v - mu
    var = jnp.mean(vc * vc, axis=-1, keepdims=True)
    vn = vc * lax.rsqrt(var + EPS) * lng_ref[...] + lnb_ref[...]
    vn_ref[...] = vn.astype(_BF16)

    r = lax.broadcasted_iota(jnp.int32, (blk, blk), 0) // CHUNK
    c = lax.broadcasted_iota(jnp.int32, (blk, blk), 1) // CHUNK
    mask = (r >= c).astype(_F32)

    for g in range(groups):
        lo, hi = g * gd, (g + 1) * gd
        u = _gelu(_dot(h, win_ref[:, lo:hi]))
        z = _dot(h, win_ref[:, 2 * e + lo:2 * e + hi])
        wm = (ws_ref[g] * mask).astype(_BF16)
        bcol = bs_ref[:, g:g + 1]
        s = jnp.concatenate(
            [_dot(wm, vn_ref[b * blk:(b + 1) * blk, lo:hi]) + bcol
             for b in range(tm // blk)], axis=0)
        y_ref[:, lo:hi] = (u * s * _silu(z)).astype(_BF16)

    o_ref[...] = x + _dot(y_ref[...], wout_ref[...])


def _gmlp_layer(x, g, w_in, ln_g, ln_b, w_s, b_s_t, w_out):
    m, d = x.shape
    e = w_out.shape[0]
    groups, blk = w_s.shape[0], w_s.shape[1]
    tm = GMLP_TILE
    kern = functools.partial(_gmlp_kernel, e=e, groups=groups, blk=blk)
    return pl.pallas_call(
        kern,
        grid=(m // tm,),
        in_specs=[
            pl.BlockSpec((tm, d), lambda i: (i, 0)),
            _const_spec((1, d)),
            _const_spec((d, 3 * e)),
            _const_spec((1, e)),
            _const_spec((1, e)),
            _const_spec((groups, blk, blk)),
            _const_spec((blk, groups)),
            _const_spec((e, d)),
        ],
        out_specs=pl.BlockSpec((tm, d), lambda i: (i, 0)),
        out_shape=jax.ShapeDtypeStruct((m, d), _F32),
        scratch_shapes=[pltpu.VMEM((tm, e), _BF16), pltpu.VMEM((tm, e), _BF16)],
        compiler_params=_params(1),
        name="gmlp_layer",
    )(x, g, w_in, ln_g, ln_b, w_s, b_s_t, w_out)


def _proj_kernel(x_ref, g_ref, w_ref, a_ref, b_ref, *, e, scale_a):
    h = _rmsnorm(x_ref[...], g_ref[...]).astype(_BF16)
    a = _dot(h, w_ref[:, :e])
    if scale_a != 1.0:
        a = a * scale_a
    a_ref[...] = a.astype(_BF16)
    b_ref[...] = _dot(h, w_ref[:, e:]).astype(_BF16)


def _norm_proj(x, g, w, scale_a=1.0):
    m, d = x.shape
    e = w.shape[1] // 2
    tm = PROJ_TILE
    kern = functools.partial(_proj_kernel, e=e, scale_a=scale_a)
    out = jax.ShapeDtypeStruct((m, e), _BF16)
    return pl.pallas_call(
        kern,
        grid=(m // tm,),
        in_specs=[
            pl.BlockSpec((tm, d), lambda i: (i, 0)),
            _const_spec((1, d)),
            _const_spec((d, 2 * e)),
        ],
        out_specs=[pl.BlockSpec((tm, e), lambda i: (i, 0))] * 2,
        out_shape=[out, out],
        compiler_params=_params(1),
        name="norm_proj",
    )(x, g, w)


def _att_kernel(q_ref, k_ref, v_ref, z_ref, y_ref, u_ref, carry_ref, acc_ref, *, tq, tk):
    s_len = q_ref.shape[0]
    row = lax.broadcasted_iota(jnp.int32, (tk, tk), 0)
    col = lax.broadcasted_iota(jnp.int32, (tk, tk), 1)
    u_ref[...] = (row > col).astype(_BF16)

    def q_block(i, c):
        t0 = pl.multiple_of(i * tq, tq)
        q = q_ref[pl.ds(t0, tq), :]
        t_idx = t0 + lax.broadcasted_iota(jnp.int32, (tq, 1), 0)
        carry_ref[...] = jnp.zeros_like(carry_ref)
        acc_ref[...] = jnp.zeros_like(acc_ref)

        def live(st):
            prev_start, worst = st
            return jnp.logical_and(prev_start > 0, worst > ATT_DEAD_LOG)

        def window(st):
            prev_start, _ = st
            start = pl.multiple_of(jnp.maximum(prev_start - tk, 0), 128)
            kw = k_ref[pl.ds(start, tk), :]
            logit = lax.dot_general(q, kw, (((1,), (1,)), ((), ())),
                                    preferred_element_type=_F32)
            s_idx = start + lax.broadcasted_iota(jnp.int32, (tq, tk), 1)
            logit = jnp.where(s_idx < jnp.minimum(t_idx, prev_start), logit, MASKED_LOGIT)
            sp = jnp.log(1.0 + jnp.exp(-jnp.abs(logit)))
            log_beta = jnp.minimum(logit, 0.0) - sp
            log_1m = jnp.minimum(-logit, 0.0) - sp
            hi = log_1m.astype(_BF16)
            lo = (log_1m - hi.astype(_F32)).astype(_BF16)
            u = u_ref[...]
            after = _dot(hi, u) + _dot(lo, u) + carry_ref[...]
            att = jnp.exp(log_beta + after).astype(_BF16)
            acc_ref[...] += _dot(att, v_ref[pl.ds(start, tk), :])
            new_carry = after[:, 0:1] + log_1m[:, 0:1]
            carry_ref[...] = new_carry
            return start, jnp.max(new_carry)

        lax.while_loop(live, window, (t0 + tq, jnp.float32(0.0)))
        z = z_ref[pl.ds(t0, tq), :].astype(_F32)
        y_ref[pl.ds(t0, tq), :] = (acc_ref[...] * _silu(z)).astype(_BF16)
        return c

    lax.fori_loop(0, s_len // tq, q_block, 0)


def _stick_breaking(q, k, v, z, batch, heads):
    m, e = q.shape
    s_len = m // batch
    dh = e // heads
    spec = pl.BlockSpec((s_len, dh), lambda b, h: (b, h))
    kern = functools.partial(_att_kernel, tq=ATT_TQ, tk=ATT_TK)
    return pl.pallas_call(
        kern,
        grid=(batch, heads),
        in_specs=[spec] * 4,
        out_specs=spec,
        out_shape=jax.ShapeDtypeStruct((m, e), _BF16),
        scratch_shapes=[pltpu.VMEM((ATT_TK, ATT_TK), _BF16),
                        pltpu.VMEM((ATT_TQ, 1), _F32),
                        pltpu.VMEM((ATT_TQ, dh), _F32)],
        compiler_params=_params(2),
        name="stick_breaking",
    )(q, k, v, z)


def _out_kernel(x_ref, y_ref, w_ref, *rest, final):
    o = x_ref[...] + _dot(y_ref[...], w_ref[...])
    if final:
        gf_ref, o_ref = rest
        o = _rmsnorm(o, gf_ref[...])
    else:
        (o_ref,) = rest
    o_ref[...] = o


def _out_proj(x, y, w, final_g=None):
    m, d = x.shape
    e = y.shape[1]
    tm = PROJ_TILE
    final = final_g is not None
    in_specs = [pl.BlockSpec((tm, d), lambda i: (i, 0)),
                pl.BlockSpec((tm, e), lambda i: (i, 0)),
                _const_spec((e, d))]
    args = [x, y, w]
    if final:
        in_specs.append(_const_spec((1, d)))
        args.append(final_g)
    return pl.pallas_call(
        functools.partial(_out_kernel, final=final),
        grid=(m // tm,),
        in_specs=in_specs,
        out_specs=pl.BlockSpec((tm, d), lambda i: (i, 0)),
        out_shape=jax.ShapeDtypeStruct((m, d), _F32),
        compiler_params=_params(1),
        name="out_proj",
    )(*args)


def kernel(x, a_norm, a_w_in, a_ln_g, a_ln_b, a_w_s, a_b_s, a_w_out, kv_norm, w_kv,
           b_norm, b_w_in, b_w_out, final_norm):
    batch, s_len, d = x.shape
    e = w_kv.shape[1] // 2
    n_a, n_b = a_w_in.shape[0], b_w_in.shape[0]
    assert a_w_s.shape[1:] == (GMLP_GROUPS, GMLP_BLOCK, GMLP_BLOCK)
    assert (batch * s_len) % PROJ_TILE == 0 and s_len % ATT_TK == 0 and GMLP_TILE % GMLP_BLOCK == 0

    xs = x.reshape(batch * s_len, d)
    for i in range(n_a):
        xs = _gmlp_layer(xs, a_norm[i][None], a_w_in[i].astype(_BF16), a_ln_g[i][None],
                         a_ln_b[i][None], a_w_s[i], a_b_s[i].T, a_w_out[i].astype(_BF16))

    k, v = _norm_proj(xs, kv_norm[None], w_kv.astype(_BF16))
    scale = 1.0 / math.sqrt(e // SB_HEADS)
    for i in range(n_b):
        q, z = _norm_proj(xs, b_norm[i][None], b_w_in[i].astype(_BF16), scale_a=scale)
        y = _stick_breaking(q, k, v, z, batch, SB_HEADS)
        xs = _out_proj(xs, y, b_w_out[i].astype(_BF16),
                       final_g=final_norm[None] if i == n_b - 1 else None)
    return xs.reshape(batch, s_len, d)
```

```python
import functools
import math

import jax
import jax.numpy as jnp
from jax import lax
from jax.experimental import pallas as pl
from jax.experimental.pallas import tpu as pltpu

EPS = 1e-6
CHUNK = 64
GMLP_BLOCK = 128
GMLP_GROUPS = 8
SB_HEADS = 16

VMEM_LIMIT_BYTES_V7X = 56 * 1024 * 1024

GMLP_TILE = 256
PROJ_TILE = 512
ATT_BLOCK = 256
ATT_HEADS_PER_STEP = 4

ATT_DEAD_LOG = -105.0
MASKED_LOGIT = -1e30
LOG2_E = math.log2(math.e)

_BF16 = jnp.bfloat16
_F32 = jnp.float32


def _dot(a, b):
    return jnp.dot(a, b, preferred_element_type=_F32)


def _gelu(x):
    c = math.sqrt(2.0 / math.pi)
    return x * (0.5 * (1.0 + jnp.tanh(c * (x + 0.044715 * (x * x * x)))))


def _silu(z):
    return z * (1.0 / (1.0 + jnp.exp(-z)))


def _rmsnorm(x, g):
    return x * lax.rsqrt(jnp.mean(x * x, axis=-1, keepdims=True) + EPS) * g


def _const_spec(shape):
    nd = len(shape)
    return pl.BlockSpec(shape, lambda *_: (0,) * nd, pipeline_mode=pl.Buffered(1))


def _params(n_axes):
    return pltpu.CompilerParams(
        dimension_semantics=("arbitrary",) * n_axes,
        vmem_limit_bytes=VMEM_LIMIT_BYTES_V7X)


def _gmlp_kernel(x_ref, g_ref, win_ref, lng_ref, lnb_ref, ws_ref, bs_ref, wout_ref,
                 o_ref, vn_ref, y_ref, *, e, groups, blk):
    tm = x_ref.shape[0]
    gd = e // groups
    x = x_ref[...]
    h = _rmsnorm(x, g_ref[...]).astype(_BF16)

    v = _gelu(_dot(h, win_ref[:, e:2 * e]))
    mu = jnp.mean(v, axis=-1, keepdims=True)
    vc = v - mu
    var = jnp.mean(vc * vc, axis=-1, keepdims=True)
    vn = vc * lax.rsqrt(var + EPS) * lng_ref[...] + lnb_ref[...]
    vn_ref[...] = vn.astype(_BF16)

    r = lax.broadcasted_iota(jnp.int32, (blk, blk), 0) // CHUNK
    c = lax.broadcasted_iota(jnp.int32, (blk, blk), 1) // CHUNK
    mask = (r >= c).astype(_F32)

    for g in range(groups):
        lo, hi = g * gd, (g + 1) * gd
        u = _gelu(_dot(h, win_ref[:, lo:hi]))
        z = _dot(h, win_ref[:, 2 * e + lo:2 * e + hi])
        wm = (ws_ref[g] * mask).astype(_BF16)
        bcol = bs_ref[:, g:g + 1]
        s = jnp.concatenate(
            [_dot(wm, vn_ref[b * blk:(b + 1) * blk, lo:hi]) + bcol
             for b in range(tm // blk)], axis=0)
        y_ref[:, lo:hi] = (u * s * _silu(z)).astype(_BF16)

    o_ref[...] = x + _dot(y_ref[...], wout_ref[...])


def _gmlp_layer(x, g, w_in, ln_g, ln_b, w_s, b_s_t, w_out):
    m, d = x.shape
    e = w_out.shape[0]
    groups, blk = w_s.shape[0], w_s.shape[1]
    tm = GMLP_TILE
    kern = functools.partial(_gmlp_kernel, e=e, groups=groups, blk=blk)
    return pl.pallas_call(
        kern,
        grid=(m // tm,),
        in_specs=[
            pl.BlockSpec((tm, d), lambda i: (i, 0)),
            _const_spec((1, d)),
            _const_spec((d, 3 * e)),
            _const_spec((1, e)),
            _const_spec((1, e)),
            _const_spec((groups, blk, blk)),
            _const_spec((blk, groups)),
            _const_spec((e, d)),
        ],
        out_specs=pl.BlockSpec((tm, d), lambda i: (i, 0)),
        out_shape=jax.ShapeDtypeStruct((m, d), _F32),
        scratch_shapes=[pltpu.VMEM((tm, e), _BF16), pltpu.VMEM((tm, e), _BF16)],
        compiler_params=_params(1),
        name="gmlp_layer",
    )(x, g, w_in, ln_g, ln_b, w_s, b_s_t, w_out)


def _proj_kernel(x_ref, g_ref, w_ref, a_ref, b_ref, *, e, scale_a):
    h = _rmsnorm(x_ref[...], g_ref[...]).astype(_BF16)
    a = _dot(h, w_ref[:, :e])
    if scale_a != 1.0:
        a = a * scale_a
    a_ref[...] = a.astype(_BF16)
    b_ref[...] = _dot(h, w_ref[:, e:]).astype(_BF16)


def _norm_proj(x, g, w, scale_a=1.0):
    m, d = x.shape
    e = w.shape[1] // 2
    tm = PROJ_TILE
    kern = functools.partial(_proj_kernel, e=e, scale_a=scale_a)
    out = jax.ShapeDtypeStruct((m, e), _BF16)
    return pl.pallas_call(
        kern,
        grid=(m // tm,),
        in_specs=[
            pl.BlockSpec((tm, d), lambda i: (i, 0)),
            _const_spec((1, d)),
            _const_spec((d, 2 * e)),
        ],
        out_specs=[pl.BlockSpec((tm, e), lambda i: (i, 0))] * 2,
        out_shape=[out, out],
        compiler_params=_params(1),
        name="norm_proj",
    )(x, g, w)


def _att_window(q, kw, vw, u2, carry, bias):
    logit = lax.dot_general(q, kw, (((1,), (1,)), ((), ())),
                            preferred_element_type=_F32)
    if bias is not None:
        logit = logit + bias
    sp = jnp.log(1.0 + jnp.exp2(jnp.abs(logit) * (-LOG2_E)))
    log_beta = jnp.minimum(logit, 0.0) - sp
    log_1m = log_beta - logit
    hi = log_1m.astype(_BF16)
    lo = (log_1m - hi.astype(_F32)).astype(_BF16)
    after = _dot(jnp.concatenate([hi, lo], axis=1), u2)
    if carry is not None:
        after = after + carry
    att = jnp.exp(log_beta + after).astype(_BF16)
    return _dot(att, vw), after[:, 0:1] + log_1m[:, 0:1]


def _att_kernel(q_ref, k_ref, v_ref, z_ref, y_ref, u2_ref, bias_ref, carry_ref, acc_ref,
                *, blk, dh):
    s_len = q_ref.shape[0]
    heads = range(q_ref.shape[1] // dh)
    row = lax.broadcasted_iota(jnp.int32, (2 * blk, blk), 0) % blk
    col = lax.broadcasted_iota(jnp.int32, (2 * blk, blk), 1)
    u2_ref[...] = (row > col).astype(_BF16)
    row = lax.broadcasted_iota(jnp.int32, (blk, blk), 0)
    col = lax.broadcasted_iota(jnp.int32, (blk, blk), 1)
    bias_ref[...] = jnp.where(col < row, 0.0, MASKED_LOGIT)

    def tile(ref, start, j):
        return ref[pl.ds(start, blk), j * dh:(j + 1) * dh]

    def gate_store(t0, j, acc):
        z = tile(z_ref, t0, j).astype(_F32)
        y_ref[pl.ds(t0, blk), j * dh:(j + 1) * dh] = (acc * _silu(z)).astype(_BF16)

    for j in heads:
        acc, _ = _att_window(tile(q_ref, 0, j), tile(k_ref, 0, j), tile(v_ref, 0, j),
                             u2_ref[...], None, bias_ref[...])
        gate_store(0, j, acc)

    def q_block(i, c):
        t0 = pl.multiple_of(i * blk, blk)
        p1 = pl.multiple_of(t0 - blk, blk)
        u2 = u2_ref[...]
        accs, carries = [], []
        for j in heads:
            q = tile(q_ref, t0, j)
            acc0, carry = _att_window(q, tile(k_ref, t0, j), tile(v_ref, t0, j), u2, None,
                                      bias_ref[...])
            acc1, carry = _att_window(q, tile(k_ref, p1, j), tile(v_ref, p1, j), u2, carry,
                                      None)
            gate_store(t0, j, acc0 + acc1)
            accs.append(acc0 + acc1)
            carries.append(carry)
        worst = jnp.max(functools.reduce(jnp.maximum, carries))

        @pl.when(jnp.logical_and(worst > ATT_DEAD_LOG, p1 > 0))
        def _():
            for j in heads:
                acc_ref[j] = accs[j]
                carry_ref[j] = carries[j]

            def live(st):
                prev_start, w = st
                return jnp.logical_and(prev_start > 0, w > ATT_DEAD_LOG)

            def window(st):
                prev_start, _ = st
                start = pl.multiple_of(prev_start - blk, blk)
                new = []
                for j in heads:
                    contrib, new_carry = _att_window(
                        tile(q_ref, t0, j), tile(k_ref, start, j), tile(v_ref, start, j),
                        u2_ref[...], carry_ref[j], None)
                    acc_ref[j] += contrib
                    carry_ref[j] = new_carry
                    new.append(new_carry)
                return start, jnp.max(functools.reduce(jnp.maximum, new))

            lax.while_loop(live, window, (p1, worst))
            for j in heads:
                gate_store(t0, j, acc_ref[j])

        return c

    lax.fori_loop(1, s_len // blk, q_block, 0)


def _stick_breaking(q, k, v, z, batch, heads):
    m, e = q.shape
    s_len = m // batch
    dh = e // heads
    blk, nh = ATT_BLOCK, ATT_HEADS_PER_STEP
    spec = pl.BlockSpec((s_len, nh * dh), lambda b, h: (b, h))
    return pl.pallas_call(
        functools.partial(_att_kernel, blk=blk, dh=dh),
        grid=(batch, heads // nh),
        in_specs=[spec] * 4,
        out_specs=spec,
        out_shape=jax.ShapeDtypeStruct((m, e), _BF16),
        scratch_shapes=[pltpu.VMEM((2 * blk, blk), _BF16),
                        pltpu.VMEM((blk, blk), _F32),
                        pltpu.VMEM((nh, blk, 1), _F32),
                        pltpu.VMEM((nh, blk, dh), _F32)],
        compiler_params=_params(2),
        name="stick_breaking",
    )(q, k, v, z)


def _out_kernel(x_ref, y_ref, w_ref, *rest, final):
    o = x_ref[...] + _dot(y_ref[...], w_ref[...])
    if final:
        gf_ref, o_ref = rest
        o = _rmsnorm(o, gf_ref[...])
    else:
        (o_ref,) = rest
    o_ref[...] = o


def _out_proj(x, y, w, final_g=None):
    m, d = x.shape
    e = y.shape[1]
    tm = PROJ_TILE
    final = final_g is not None
    in_specs = [pl.BlockSpec((tm, d), lambda i: (i, 0)),
                pl.BlockSpec((tm, e), lambda i: (i, 0)),
                _const_spec((e, d))]
    args = [x, y, w]
    if final:
        in_specs.append(_const_spec((1, d)))
        args.append(final_g)
    return pl.pallas_call(
        functools.partial(_out_kernel, final=final),
        grid=(m // tm,),
        in_specs=in_specs,
        out_specs=pl.BlockSpec((tm, d), lambda i: (i, 0)),
        out_shape=jax.ShapeDtypeStruct((m, d), _F32),
        compiler_params=_params(1),
        name="out_proj",
    )(*args)


def kernel(x, a_norm, a_w_in, a_ln_g, a_ln_b, a_w_s, a_b_s, a_w_out, kv_norm, w_kv,
           b_norm, b_w_in, b_w_out, final_norm):
    batch, s_len, d = x.shape
    e = w_kv.shape[1] // 2
    n_a, n_b = a_w_in.shape[0], b_w_in.shape[0]
    assert n_b >= 1 and a_w_s.shape[1:] == (GMLP_GROUPS, GMLP_BLOCK, GMLP_BLOCK)
    assert (batch * s_len) % PROJ_TILE == 0 and s_len % ATT_BLOCK == 0
    assert GMLP_TILE % GMLP_BLOCK == 0 and SB_HEADS % ATT_HEADS_PER_STEP == 0

    xs = x.reshape(batch * s_len, d)
    for i in range(n_a):
        xs = _gmlp_layer(xs, a_norm[i][None], a_w_in[i].astype(_BF16), a_ln_g[i][None],
                         a_ln_b[i][None], a_w_s[i], a_b_s[i].T, a_w_out[i].astype(_BF16))

    k, v = _norm_proj(xs, kv_norm[None], w_kv.astype(_BF16))
    scale = 1.0 / math.sqrt(e // SB_HEADS)
    for i in range(n_b):
        q, z = _norm_proj(xs, b_norm[i][None], b_w_in[i].astype(_BF16), scale_a=scale)
        y = _stick_breaking(q, k, v, z, batch, SB_HEADS)
        xs = _out_proj(xs, y, b_w_out[i].astype(_BF16),
                       final_g=final_norm[None] if i == n_b - 1 else None)
    return xs.reshape(batch, s_len, d)
```

```python
import functools
import math

import jax
import jax.numpy as jnp
from jax import lax
from jax.experimental import pallas as pl
from jax.experimental.pallas import tpu as pltpu

EPS = 1e-6
CHUNK = 64
GMLP_BLOCK = 128
GMLP_GROUPS = 8
SB_HEADS = 16

VMEM_LIMIT_BYTES_V7X = 56 * 1024 * 1024

GMLP_TILE = 256
PROJ_TILE = 512
ATT_BLOCK = 256
ATT_HEADS_PER_STEP = 4

ATT_DEAD_LOG = -105.0
MASKED_LOGIT = -1e30
LOG2_E = math.log2(math.e)

_BF16 = jnp.bfloat16
_F32 = jnp.float32


def _dot(a, b):
    return jnp.dot(a, b, preferred_element_type=_F32)


def _gelu(x):
    c = math.sqrt(2.0 / math.pi)
    return x * (0.5 * (1.0 + jnp.tanh(c * (x + 0.044715 * (x * x * x)))))


def _silu(z):
    return z * (1.0 / (1.0 + jnp.exp(-z)))


def _rmsnorm(x, g):
    return x * lax.rsqrt(jnp.mean(x * x, axis=-1, keepdims=True) + EPS) * g


def _const_spec(shape):
    nd = len(shape)
    return pl.BlockSpec(shape, lambda *_: (0,) * nd, pipeline_mode=pl.Buffered(1))


def _params(n_axes):
    return pltpu.CompilerParams(
        dimension_semantics=("arbitrary",) * n_axes,
        vmem_limit_bytes=VMEM_LIMIT_BYTES_V7X)


def _gmlp_kernel(x_ref, g_ref, win_ref, lng_ref, lnb_ref, ws_ref, bs_ref, wout_ref,
                 o_ref, vn_ref, y_ref, *, e, groups, blk):
    tm = x_ref.shape[0]
    gd = e // groups
    x = x_ref[...]
    h = _rmsnorm(x, g_ref[...]).astype(_BF16)

    v = _gelu(_dot(h, win_ref[:, e:2 * e]))
    mu = jnp.mean(v, axis=-1, keepdims=True)
    vc = v - mu
    var = jnp.mean(vc * vc, axis=-1, keepdims=True)
    vn = vc * lax.rsqrt(var + EPS) * lng_ref[...] + lnb_ref[...]
    vn_ref[...] = vn.astype(_BF16)

    r = lax.broadcasted_iota(jnp.int32, (blk, blk), 0) // CHUNK
    c = lax.broadcasted_iota(jnp.int32, (blk, blk), 1) // CHUNK
    mask = (r >= c).astype(_F32)

    for g in range(groups):
        lo, hi = g * gd, (g + 1) * gd
        u = _gelu(_dot(h, win_ref[:, lo:hi]))
        z = _dot(h, win_ref[:, 2 * e + lo:2 * e + hi])
        wm = (ws_ref[g] * mask).astype(_BF16)
        bcol = bs_ref[:, g:g + 1]
        s = jnp.concatenate(
            [_dot(wm, vn_ref[b * blk:(b + 1) * blk, lo:hi]) + bcol
             for b in range(tm // blk)], axis=0)
        y_ref[:, lo:hi] = (u * s * _silu(z)).astype(_BF16)

    o_ref[...] = x + _dot(y_ref[...], wout_ref[...])


def _gmlp_layer(x, g, w_in, ln_g, ln_b, w_s, b_s_t, w_out):
    m, d = x.shape
    e = w_out.shape[0]
    groups, blk = w_s.shape[0], w_s.shape[1]
    tm = GMLP_TILE
    kern = functools.partial(_gmlp_kernel, e=e, groups=groups, blk=blk)
    return pl.pallas_call(
        kern,
        grid=(m // tm,),
        in_specs=[
            pl.BlockSpec((tm, d), lambda i: (i, 0)),
            _const_spec((1, d)),
            _const_spec((d, 3 * e)),
            _const_spec((1, e)),
            _const_spec((1, e)),
            _const_spec((groups, blk, blk)),
            _const_spec((blk, groups)),
            _const_spec((e, d)),
        ],
        out_specs=pl.BlockSpec((tm, d), lambda i: (i, 0)),
        out_shape=jax.ShapeDtypeStruct((m, d), _F32),
        scratch_shapes=[pltpu.VMEM((tm, e), _BF16), pltpu.VMEM((tm, e), _BF16)],
        compiler_params=_params(1),
        name="gmlp_layer",
    )(x, g, w_in, ln_g, ln_b, w_s, b_s_t, w_out)


def _proj_kernel(x_ref, g_ref, w_ref, a_ref, b_ref, *, e, scale_a):
    h = _rmsnorm(x_ref[...], g_ref[...]).astype(_BF16)
    a = _dot(h, w_ref[:, :e])
    if scale_a != 1.0:
        a = a * scale_a
    a_ref[...] = a.astype(_BF16)
    b_ref[...] = _dot(h, w_ref[:, e:]).astype(_BF16)


def _norm_proj(x, g, w, scale_a=1.0):
    m, d = x.shape
    e = w.shape[1] // 2
    tm = PROJ_TILE
    kern = functools.partial(_proj_kernel, e=e, scale_a=scale_a)
    out = jax.ShapeDtypeStruct((m, e), _BF16)
    return pl.pallas_call(
        kern,
        grid=(m // tm,),
        in_specs=[
            pl.BlockSpec((tm, d), lambda i: (i, 0)),
            _const_spec((1, d)),
            _const_spec((d, 2 * e)),
        ],
        out_specs=[pl.BlockSpec((tm, e), lambda i: (i, 0))] * 2,
        out_shape=[out, out],
        compiler_params=_params(1),
        name="norm_proj",
    )(x, g, w)


def _att_logits(q, kw):
    return lax.dot_general(q, kw, (((1,), (1,)), ((), ())),
                           preferred_element_type=_F32)


def _att_weights(logit, u2, carry):
    sp = jnp.log(1.0 + jnp.exp2(jnp.abs(logit) * (-LOG2_E)))
    log_beta = jnp.minimum(logit, 0.0) - sp
    log_1m = log_beta - logit
    hi = log_1m.astype(_BF16)
    lo = (log_1m - hi.astype(_F32)).astype(_BF16)
    after = _dot(jnp.concatenate([hi, lo], axis=1), u2)
    if carry is not None:
        after = after + carry
    att = jnp.exp(log_beta + after).astype(_BF16)
    return att, after[:, 0:1] + log_1m[:, 0:1]


def _att_kernel(q_ref, k_ref, v_ref, z_ref, y_ref, u2_ref, u2h_ref, bias_ref, carry_ref, acc_ref,
                *head_refs, blk, dh):
    s_len = q_ref.shape[0]
    n_blk = s_len // blk
    half = blk // 2
    heads = range(q_ref.shape[1] // dh)
    lgt_ref, lgb_ref, lgp_ref, att_ref = (head_refs[n::4] for n in range(4))

    def suffix_sum_matrix(keys):
        row = lax.broadcasted_iota(jnp.int32, (2 * keys, keys), 0) % keys
        col = lax.broadcasted_iota(jnp.int32, (2 * keys, keys), 1)
        return (row > col).astype(_BF16)

    u2_ref[...] = suffix_sum_matrix(blk)
    u2h_ref[...] = suffix_sum_matrix(half)
    row = lax.broadcasted_iota(jnp.int32, (blk, blk), 0)
    col = lax.broadcasted_iota(jnp.int32, (blk, blk), 1)
    bias_ref[...] = jnp.where(col < row, 0.0, MASKED_LOGIT)

    def tile(ref, start, j, rows=blk):
        return ref[pl.ds(start, rows), j * dh:(j + 1) * dh]

    def gate_store(t0, j, acc):
        z = tile(z_ref, t0, j).astype(_F32)
        y_ref[pl.ds(t0, blk), j * dh:(j + 1) * dh] = (acc * _silu(z)).astype(_BF16)

    def store_logits(t0, tp, j):
        q = tile(q_ref, t0, j)
        lgt_ref[j][...] = _att_logits(q[:half], tile(k_ref, t0, j, half))
        lgb_ref[j][...] = _att_logits(q[half:], tile(k_ref, t0, j))
        if tp is not None:
            lgp_ref[j][...] = _att_logits(q, tile(k_ref, tp, j))

    def store_weights(j, with_prev):
        att_t, c_t = _att_weights(lgt_ref[j][...] + bias_ref[:half, :half], u2h_ref[...], None)
        att_b, c_b = _att_weights(lgb_ref[j][...] + bias_ref[half:, :], u2_ref[...], None)
        att_ref[j][0, :half, :half] = att_t
        att_ref[j][0, half:, :] = att_b
        carry = jnp.concatenate([c_t, c_b], axis=0)
        if with_prev:
            att_ref[j][1], carry = _att_weights(lgp_ref[j][...], u2_ref[...], carry)
        else:
            att_ref[j][1] = jnp.zeros((blk, blk), _BF16)
        return carry

    def block_output(t0, p1, j):
        top = _dot(att_ref[j][0, :half, :half], tile(v_ref, t0, j, half))
        bot = _dot(att_ref[j][0, half:, :], tile(v_ref, t0, j))
        return jnp.concatenate([top, bot], axis=0) + _dot(att_ref[j][1], tile(v_ref, p1, j))

    def finish_block(t0, p1, j):
        acc = block_output(t0, p1, j)
        acc_ref[j] = acc
        gate_store(t0, j, acc)

    def walk_back(t0, slot, worst):
        @pl.when(jnp.logical_and(worst > ATT_DEAD_LOG, t0 - blk > 0))
        def _():
            def live(st):
                prev_start, w = st
                return jnp.logical_and(prev_start > 0, w > ATT_DEAD_LOG)

            def window(st):
                prev_start, _ = st
                start = pl.multiple_of(prev_start - blk, blk)
                new = []
                for j in heads:
                    logit = _att_logits(tile(q_ref, t0, j), tile(k_ref, start, j))
                    att, new_carry = _att_weights(logit, u2_ref[...], carry_ref[slot, j])
                    acc_ref[j] += _dot(att, tile(v_ref, start, j))
                    carry_ref[slot, j] = new_carry
                    new.append(new_carry)
                return start, jnp.max(functools.reduce(jnp.maximum, new))

            lax.while_loop(live, window, (t0 - blk, worst))
            for j in heads:
                gate_store(t0, j, acc_ref[j])

    for j in heads:
        store_logits(0, None, j)
    for j in heads:
        store_weights(j, with_prev=False)
    for j in heads:
        store_logits(blk, 0, j)

    def step(i, worst_prev):
        t0 = pl.multiple_of(i * blk, blk)
        tp = pl.multiple_of(t0 - blk, blk)
        pp = pl.multiple_of(jnp.maximum(tp - blk, 0), blk)
        tn = pl.multiple_of(jnp.minimum(t0 + blk, s_len - blk), blk)
        slot = jnp.bitwise_and(i, 1)
        carries = []
        for j in heads:
            finish_block(tp, pp, j)
            carry = store_weights(j, with_prev=True)
            carry_ref[slot, j] = carry
            carries.append(carry)
            store_logits(tn, pl.multiple_of(tn - blk, blk), j)
        walk_back(tp, 1 - slot, worst_prev)
        return jnp.max(functools.reduce(jnp.maximum, carries))

    worst = lax.fori_loop(1, n_blk, step, jnp.float32(2.0 * ATT_DEAD_LOG))
    t_last = (n_blk - 1) * blk
    for j in heads:
        finish_block(t_last, t_last - blk, j)
    walk_back(t_last, (n_blk - 1) % 2, worst)


def _stick_breaking(q, k, v, z, batch, heads):
    m, e = q.shape
    s_len = m // batch
    dh = e // heads
    blk, nh = ATT_BLOCK, ATT_HEADS_PER_STEP
    spec = pl.BlockSpec((s_len, nh * dh), lambda b, h: (b, h))
    per_head = [pltpu.VMEM((blk // 2, blk // 2), _F32),
                pltpu.VMEM((blk // 2, blk), _F32),
                pltpu.VMEM((blk, blk), _F32),
                pltpu.VMEM((2, blk, blk), _BF16)]
    return pl.pallas_call(
        functools.partial(_att_kernel, blk=blk, dh=dh),
        grid=(batch, heads // nh),
        in_specs=[spec] * 4,
        out_specs=spec,
        out_shape=jax.ShapeDtypeStruct((m, e), _BF16),
        scratch_shapes=[pltpu.VMEM((2 * blk, blk), _BF16),
                        pltpu.VMEM((blk, blk // 2), _BF16),
                        pltpu.VMEM((blk, blk), _F32),
                        pltpu.VMEM((2, nh, blk, 1), _F32),
                        pltpu.VMEM((nh, blk, dh), _F32)] + per_head * nh,
        compiler_params=_params(2),
        name="stick_breaking",
    )(q, k, v, z)


def _out_kernel(x_ref, y_ref, w_ref, *rest, final):
    o = x_ref[...] + _dot(y_ref[...], w_ref[...])
    if final:
        gf_ref, o_ref = rest
        o = _rmsnorm(o, gf_ref[...])
    else:
        (o_ref,) = rest
    o_ref[...] = o


def _out_proj(x, y, w, final_g=None):
    m, d = x.shape
    e = y.shape[1]
    tm = PROJ_TILE
    final = final_g is not None
    in_specs = [pl.BlockSpec((tm, d), lambda i: (i, 0)),
                pl.BlockSpec((tm, e), lambda i: (i, 0)),
                _const_spec((e, d))]
    args = [x, y, w]
    if final:
        in_specs.append(_const_spec((1, d)))
        args.append(final_g)
    return pl.pallas_call(
        functools.partial(_out_kernel, final=final),
        grid=(m // tm,),
        in_specs=in_specs,
        out_specs=pl.BlockSpec((tm, d), lambda i: (i, 0)),
        out_shape=jax.ShapeDtypeStruct((m, d), _F32),
        compiler_params=_params(1),
        name="out_proj",
    )(*args)


def kernel(x, a_norm, a_w_in, a_ln_g, a_ln_b, a_w_s, a_b_s, a_w_out, kv_norm, w_kv,
           b_norm, b_w_in, b_w_out, final_norm):
    batch, s_len, d = x.shape
    e = w_kv.shape[1] // 2
    n_a, n_b = a_w_in.shape[0], b_w_in.shape[0]
    assert n_b >= 1 and a_w_s.shape[1:] == (GMLP_GROUPS, GMLP_BLOCK, GMLP_BLOCK)
    assert (batch * s_len) % PROJ_TILE == 0 and s_len % ATT_BLOCK == 0 and s_len >= 2 * ATT_BLOCK
    assert GMLP_TILE % GMLP_BLOCK == 0 and SB_HEADS % ATT_HEADS_PER_STEP == 0

    xs = x.reshape(batch * s_len, d)
    for i in range(n_a):
        xs = _gmlp_layer(xs, a_norm[i][None], a_w_in[i].astype(_BF16), a_ln_g[i][None],
                         a_ln_b[i][None], a_w_s[i], a_b_s[i].T, a_w_out[i].astype(_BF16))

    k, v = _norm_proj(xs, kv_norm[None], w_kv.astype(_BF16))
    scale = 1.0 / math.sqrt(e // SB_HEADS)
    for i in range(n_b):
        q, z = _norm_proj(xs, b_norm[i][None], b_w_in[i].astype(_BF16), scale_a=scale)
        y = _stick_breaking(q, k, v, z, batch, SB_HEADS)
        xs = _out_proj(xs, y, b_w_out[i].astype(_BF16),
                       final_g=final_norm[None] if i == n_b - 1 else None)
    return xs.reshape(batch, s_len, d)
```

```python
import functools
import math

import jax
import jax.numpy as jnp
from jax import lax
from jax.experimental import pallas as pl
from jax.experimental.pallas import tpu as pltpu

EPS = 1e-6
CHUNK = 64
GMLP_BLOCK = 128
GMLP_GROUPS = 8
SB_HEADS = 16

VMEM_LIMIT_BYTES_V7X = 56 * 1024 * 1024

GMLP_TILE = 256
PROJ_TILE = 512
ATT_BLOCK = 256
ATT_HEADS_PER_STEP = 4

ATT_DEAD_LOG = -105.0
MASKED_LOGIT = -1e30
LOG2_E = math.log2(math.e)

_BF16 = jnp.bfloat16
_F32 = jnp.float32


def _dot(a, b):
    return jnp.dot(a, b, preferred_element_type=_F32)


def _gelu(x):
    c = math.sqrt(2.0 / math.pi)
    hx = 0.5 * x
    return hx + hx * jnp.tanh(x * (c + (c * 0.044715) * (x * x)))


def _silu(z):
    hz = 0.5 * z
    return hz + hz * jnp.tanh(hz)


def _rmsnorm(x, g):
    return x * lax.rsqrt(jnp.mean(x * x, axis=-1, keepdims=True) + EPS) * g


def _const_spec(shape):
    nd = len(shape)
    return pl.BlockSpec(shape, lambda *_: (0,) * nd, pipeline_mode=pl.Buffered(1))


def _params(n_axes):
    return pltpu.CompilerParams(
        dimension_semantics=("arbitrary",) * n_axes,
        vmem_limit_bytes=VMEM_LIMIT_BYTES_V7X)


def _gmlp_kernel(x_ref, g_ref, win_ref, lng_ref, lnb_ref, ws_ref, bs_ref, wout_ref,
                 o_ref, vn_ref, y_ref, *, e, groups, blk):
    tm = x_ref.shape[0]
    gd = e // groups
    x = x_ref[...]
    h = _rmsnorm(x, g_ref[...]).astype(_BF16)

    v = _gelu(_dot(h, win_ref[:, e:2 * e]))
    mu = jnp.mean(v, axis=-1, keepdims=True)
    vc = v - mu
    var = jnp.mean(vc * vc, axis=-1, keepdims=True)
    vn = vc * lax.rsqrt(var + EPS) * lng_ref[...] + lnb_ref[...]
    vn_ref[...] = vn.astype(_BF16)

    r = lax.broadcasted_iota(jnp.int32, (blk, blk), 0) // CHUNK
    c = lax.broadcasted_iota(jnp.int32, (blk, blk), 1) // CHUNK
    mask = (r >= c).astype(_F32)

    for g in range(groups):
        lo, hi = g * gd, (g + 1) * gd
        u = _gelu(_dot(h, win_ref[:, lo:hi]))
        z = _dot(h, win_ref[:, 2 * e + lo:2 * e + hi])
        wm = (ws_ref[g] * mask).astype(_BF16)
        bcol = bs_ref[:, g:g + 1]
        s = jnp.concatenate(
            [_dot(wm, vn_ref[b * blk:(b + 1) * blk, lo:hi]) + bcol
             for b in range(tm // blk)], axis=0)
        y_ref[:, lo:hi] = (u * s * _silu(z)).astype(_BF16)

    o_ref[...] = x + _dot(y_ref[...], wout_ref[...])


def _gmlp_layer(x, g, w_in, ln_g, ln_b, w_s, b_s_t, w_out):
    m, d = x.shape
    e = w_out.shape[0]
    groups, blk = w_s.shape[0], w_s.shape[1]
    tm = GMLP_TILE
    kern = functools.partial(_gmlp_kernel, e=e, groups=groups, blk=blk)
    return pl.pallas_call(
        kern,
        grid=(m // tm,),
        in_specs=[
            pl.BlockSpec((tm, d), lambda i: (i, 0)),
            _const_spec((1, d)),
            _const_spec((d, 3 * e)),
            _const_spec((1, e)),
            _const_spec((1, e)),
            _const_spec((groups, blk, blk)),
            _const_spec((blk, groups)),
            _const_spec((e, d)),
        ],
        out_specs=pl.BlockSpec((tm, d), lambda i: (i, 0)),
        out_shape=jax.ShapeDtypeStruct((m, d), _F32),
        scratch_shapes=[pltpu.VMEM((tm, e), _BF16), pltpu.VMEM((tm, e), _BF16)],
        compiler_params=_params(1),
        name="gmlp_layer",
    )(x, g, w_in, ln_g, ln_b, w_s, b_s_t, w_out)


def _proj_kernel(x_ref, g_ref, w_ref, a_ref, b_ref, *, e, scale_a, silu_b):
    h = _rmsnorm(x_ref[...], g_ref[...]).astype(_BF16)
    a = _dot(h, w_ref[:, :e])
    if scale_a != 1.0:
        a = a * scale_a
    a_ref[...] = a.astype(_BF16)
    b = _dot(h, w_ref[:, e:])
    if silu_b:
        b = _silu(b)
    b_ref[...] = b.astype(_BF16)


def _norm_proj(x, g, w, scale_a=1.0, silu_b=False):
    m, d = x.shape
    e = w.shape[1] // 2
    tm = PROJ_TILE
    kern = functools.partial(_proj_kernel, e=e, scale_a=scale_a, silu_b=silu_b)
    out = jax.ShapeDtypeStruct((m, e), _BF16)
    return pl.pallas_call(
        kern,
        grid=(m // tm,),
        in_specs=[
            pl.BlockSpec((tm, d), lambda i: (i, 0)),
            _const_spec((1, d)),
            _const_spec((d, 2 * e)),
        ],
        out_specs=[pl.BlockSpec((tm, e), lambda i: (i, 0))] * 2,
        out_shape=[out, out],
        compiler_params=_params(1),
        name="norm_proj",
    )(x, g, w)


def _att_logits(q, kw):
    return lax.dot_general(q, kw, (((1,), (1,)), ((), ())),
                           preferred_element_type=_F32)


def _att_weights(logit, u, carry):
    sp = jnp.log(1.0 + jnp.exp2(jnp.abs(logit) * (-LOG2_E)))
    log_beta = jnp.minimum(logit, 0.0) - sp
    log_1m = log_beta - logit
    after = _dot(log_1m.astype(_BF16), u)
    if carry is not None:
        after = after + carry
    att = jnp.exp(log_beta + after).astype(_BF16)
    return att, after[:, 0:1] + log_1m[:, 0:1]


def _att_kernel(q_ref, k_ref, v_ref, o_ref, u_ref, ud_ref, bias_ref, carry_ref, acc_ref,
                *head_refs, blk, dh):
    s_len = q_ref.shape[0]
    n_blk = s_len // blk
    unit = blk // 2
    heads = range(q_ref.shape[1] // dh)
    lgd_ref, lgp_ref, attd_ref, attp_ref = (head_refs[n::4] for n in range(4))

    def suffix_sum_matrix(keys):
        row = lax.broadcasted_iota(jnp.int32, (keys, keys), 0)
        col = lax.broadcasted_iota(jnp.int32, (keys, keys), 1)
        return (row > col).astype(_BF16)

    u_ref[...] = suffix_sum_matrix(blk)
    ud_ref[...] = suffix_sum_matrix(unit)
    row = lax.broadcasted_iota(jnp.int32, (unit, unit), 0)
    col = lax.broadcasted_iota(jnp.int32, (unit, unit), 1)
    bias_ref[...] = jnp.where(col < row, 0.0, MASKED_LOGIT)

    def tile(ref, start, j, rows):
        return ref[pl.ds(start, rows), j * dh:(j + 1) * dh]

    def unit_rows(t0, n):
        return pl.multiple_of(t0 + n * unit, unit)

    def prev_start(r0):
        return pl.multiple_of(jnp.maximum(r0 - blk, 0), unit)

    def store_output(t0, j, acc):
        o_ref[pl.ds(t0, blk), j * dh:(j + 1) * dh] = acc.astype(_BF16)

    def store_logits(t0, j):
        for n in range(2):
            r0 = unit_rows(t0, n)
            q = tile(q_ref, r0, j, unit)
            lgd_ref[j][n] = _att_logits(q, tile(k_ref, r0, j, unit))
            lgp_ref[j][n] = _att_logits(q, tile(k_ref, prev_start(r0), j, blk))

    def store_weights(j, prev_bias=None):
        carries = []
        for n in range(2):
            att, carry = _att_weights(lgd_ref[j][n] + bias_ref[...], ud_ref[...], None)
            attd_ref[j][n] = att
            logit = lgp_ref[j][n]
            if prev_bias is not None:
                logit = logit + prev_bias[n]
            attp_ref[j][n], carry = _att_weights(logit, u_ref[...], carry)
            carries.append(carry)
        return jnp.concatenate(carries, axis=0)

    def finish_block(t0, j):
        accs = []
        for n in range(2):
            r0 = unit_rows(t0, n)
            accs.append(_dot(attd_ref[j][n], tile(v_ref, r0, j, unit))
                        + _dot(attp_ref[j][n], tile(v_ref, prev_start(r0), j, blk)))
        acc = jnp.concatenate(accs, axis=0)
        acc_ref[j] = acc
        store_output(t0, j, acc)

    def walk_back(t0, slot, worst):
        @pl.when(jnp.logical_and(worst > ATT_DEAD_LOG, t0 - unit > 0))
        def _():
            r1 = unit_rows(t0, 1)
            start1 = pl.multiple_of(t0 - blk, unit)
            new = []
            for j in heads:
                logit = _att_logits(tile(q_ref, r1, j, unit), tile(k_ref, start1, j, unit))
                att, carry = _att_weights(logit, ud_ref[...], carry_ref[slot, j, unit:, :])
                acc_ref[j, unit:, :] += _dot(att, tile(v_ref, start1, j, unit))
                carry_ref[slot, j, unit:, :] = carry
                new.append(jnp.maximum(jnp.max(carry), jnp.max(carry_ref[slot, j, :unit, :])))

            def live(st):
                start, w = st
                return jnp.logical_and(start > 0, w > ATT_DEAD_LOG)

            def window(st):
                start = pl.multiple_of(st[0] - blk, blk)
                new = []
                for j in heads:
                    logit = _att_logits(tile(q_ref, t0, j, blk), tile(k_ref, start, j, blk))
                    att, carry = _att_weights(logit, u_ref[...], carry_ref[slot, j])
                    acc_ref[j] += _dot(att, tile(v_ref, start, j, blk))
                    carry_ref[slot, j] = carry
                    new.append(jnp.max(carry))
                return start, functools.reduce(jnp.maximum, new)

            lax.while_loop(live, window, (start1, functools.reduce(jnp.maximum, new)))
            for j in heads:
                store_output(t0, j, acc_ref[j])

    col = lax.broadcasted_iota(jnp.int32, (unit, blk), 1)
    first_bias = [jnp.full((unit, blk), MASKED_LOGIT, _F32),
                  jnp.where(col < unit, 0.0, MASKED_LOGIT)]
    for j in heads:
        store_logits(0, j)
    for j in heads:
        store_weights(j, first_bias)
    for j in heads:
        store_logits(blk, j)

    def step(i, worst_prev):
        t0 = pl.multiple_of(i * blk, blk)
        tp = pl.multiple_of(t0 - blk, blk)
        tn = pl.multiple_of(jnp.minimum(t0 + blk, s_len - blk), blk)
        slot = jnp.bitwise_and(i, 1)
        carries = []
        for j in heads:
            finish_block(tp, j)
            carry = store_weights(j)
            carry_ref[slot, j] = carry
            carries.append(carry)
            store_logits(tn, j)
        walk_back(tp, 1 - slot, worst_prev)
        return jnp.max(functools.reduce(jnp.maximum, carries))

    worst = lax.fori_loop(1, n_blk, step, jnp.float32(2.0 * ATT_DEAD_LOG))
    t_last = (n_blk - 1) * blk
    for j in heads:
        finish_block(t_last, j)
    walk_back(t_last, (n_blk - 1) % 2, worst)


def _stick_breaking(q, k, v, batch, heads):
    m, e = q.shape
    s_len = m // batch
    dh = e // heads
    blk, nh = ATT_BLOCK, ATT_HEADS_PER_STEP
    unit = blk // 2
    spec = pl.BlockSpec((s_len, nh * dh), lambda b, h: (b, h))
    per_head = [pltpu.VMEM((2, unit, unit), _F32),
                pltpu.VMEM((2, unit, blk), _F32),
                pltpu.VMEM((2, unit, unit), _BF16),
                pltpu.VMEM((2, unit, blk), _BF16)]
    return pl.pallas_call(
        functools.partial(_att_kernel, blk=blk, dh=dh),
        grid=(batch, heads // nh),
        in_specs=[spec] * 3,
        out_specs=spec,
        out_shape=jax.ShapeDtypeStruct((m, e), _BF16),
        scratch_shapes=[pltpu.VMEM((blk, blk), _BF16),
                        pltpu.VMEM((unit, unit), _BF16),
                        pltpu.VMEM((unit, unit), _F32),
                        pltpu.VMEM((2, nh, blk, 1), _F32),
                        pltpu.VMEM((nh, blk, dh), _F32)] + per_head * nh,
        compiler_params=_params(2),
        name="stick_breaking",
    )(q, k, v)


def _out_kernel(x_ref, a_ref, sz_ref, w_ref, *rest, final):
    y = (a_ref[...].astype(_F32) * sz_ref[...].astype(_F32)).astype(_BF16)
    o = x_ref[...] + _dot(y, w_ref[...])
    if final:
        gf_ref, o_ref = rest
        o = _rmsnorm(o, gf_ref[...])
    else:
        (o_ref,) = rest
    o_ref[...] = o


def _out_proj(x, a, sz, w, final_g=None):
    m, d = x.shape
    e = a.shape[1]
    tm = PROJ_TILE
    final = final_g is not None
    in_specs = [pl.BlockSpec((tm, d), lambda i: (i, 0)),
                pl.BlockSpec((tm, e), lambda i: (i, 0)),
                pl.BlockSpec((tm, e), lambda i: (i, 0)),
                _const_spec((e, d))]
    args = [x, a, sz, w]
    if final:
        in_specs.append(_const_spec((1, d)))
        args.append(final_g)
    return pl.pallas_call(
        functools.partial(_out_kernel, final=final),
        grid=(m // tm,),
        in_specs=in_specs,
        out_specs=pl.BlockSpec((tm, d), lambda i: (i, 0)),
        out_shape=jax.ShapeDtypeStruct((m, d), _F32),
        compiler_params=_params(1),
        name="out_proj",
    )(*args)


def kernel(x, a_norm, a_w_in, a_ln_g, a_ln_b, a_w_s, a_b_s, a_w_out, kv_norm, w_kv,
           b_norm, b_w_in, b_w_out, final_norm):
    batch, s_len, d = x.shape
    e = w_kv.shape[1] // 2
    n_a, n_b = a_w_in.shape[0], b_w_in.shape[0]
    assert n_b >= 1 and a_w_s.shape[1:] == (GMLP_GROUPS, GMLP_BLOCK, GMLP_BLOCK)
    assert (batch * s_len) % PROJ_TILE == 0 and s_len % ATT_BLOCK == 0 and s_len >= 2 * ATT_BLOCK
    assert GMLP_TILE % GMLP_BLOCK == 0 and SB_HEADS % ATT_HEADS_PER_STEP == 0

    xs = x.reshape(batch * s_len, d)
    for i in range(n_a):
        xs = _gmlp_layer(xs, a_norm[i][None], a_w_in[i].astype(_BF16), a_ln_g[i][None],
                         a_ln_b[i][None], a_w_s[i], a_b_s[i].T, a_w_out[i].astype(_BF16))

    k, v = _norm_proj(xs, kv_norm[None], w_kv.astype(_BF16))
    scale = 1.0 / math.sqrt(e // SB_HEADS)
    for i in range(n_b):
        q, sz = _norm_proj(xs, b_norm[i][None], b_w_in[i].astype(_BF16), scale_a=scale,
                           silu_b=True)
        a = _stick_breaking(q, k, v, batch, SB_HEADS)
        xs = _out_proj(xs, a, sz, b_w_out[i].astype(_BF16),
                       final_g=final_norm[None] if i == n_b - 1 else None)
    return xs.reshape(batch, s_len, d)
```

```python
import functools
import math

import jax
import jax.numpy as jnp
from jax import lax
from jax.experimental import pallas as pl
from jax.experimental.pallas import tpu as pltpu

EPS = 1e-6
CHUNK = 64
GMLP_BLOCK = 128
GMLP_GROUPS = 8
SB_HEADS = 16

VMEM_LIMIT_BYTES_V7X = 56 * 1024 * 1024

GMLP_TILE = 256
PROJ_TILE = 512
ATT_BLOCK = 256
ATT_HEADS_PER_STEP = 4

ATT_DEAD_LOG = -105.0
MASKED_LOGIT = -1e30
LOG2_E = math.log2(math.e)

_BF16 = jnp.bfloat16
_F32 = jnp.float32


def _dot(a, b):
    return jnp.dot(a, b, preferred_element_type=_F32)


def _gelu(x):
    c = math.sqrt(2.0 / math.pi)
    hx = 0.5 * x
    return hx + hx * jnp.tanh(x * (c + (c * 0.044715) * (x * x)))


def _silu(z):
    hz = 0.5 * z
    return hz + hz * jnp.tanh(hz)


def _rmsnorm(x, g):
    return x * lax.rsqrt(jnp.mean(x * x, axis=-1, keepdims=True) + EPS) * g


def _const_spec(shape):
    nd = len(shape)
    return pl.BlockSpec(shape, lambda *_: (0,) * nd, pipeline_mode=pl.Buffered(1))


def _params(n_axes):
    return pltpu.CompilerParams(
        dimension_semantics=("arbitrary",) * n_axes,
        vmem_limit_bytes=VMEM_LIMIT_BYTES_V7X)


def _gmlp_kernel(x_ref, g_ref, win_ref, lng_ref, lnb_ref, ws_ref, bs_ref, wout_ref,
                 o_ref, vn_ref, y_ref, *, e, groups, blk):
    tm = x_ref.shape[0]
    gd = e // groups
    x = x_ref[...]
    h = _rmsnorm(x, g_ref[...]).astype(_BF16)

    v = _gelu(_dot(h, win_ref[:, e:2 * e]))
    mu = jnp.mean(v, axis=-1, keepdims=True)
    vc = v - mu
    var = jnp.mean(vc * vc, axis=-1, keepdims=True)
    vn = vc * lax.rsqrt(var + EPS) * lng_ref[...] + lnb_ref[...]
    vn_ref[...] = vn.astype(_BF16)

    r = lax.broadcasted_iota(jnp.int32, (blk, blk), 0) // CHUNK
    c = lax.broadcasted_iota(jnp.int32, (blk, blk), 1) // CHUNK
    mask = (r >= c).astype(_F32)

    for g in range(groups):
        lo, hi = g * gd, (g + 1) * gd
        u = _gelu(_dot(h, win_ref[:, lo:hi]))
        z = _dot(h, win_ref[:, 2 * e + lo:2 * e + hi])
        wm = (ws_ref[g] * mask).astype(_BF16)
        bcol = bs_ref[:, g:g + 1]
        s = jnp.concatenate(
            [_dot(wm, vn_ref[b * blk:(b + 1) * blk, lo:hi]) + bcol
             for b in range(tm // blk)], axis=0)
        y_ref[:, lo:hi] = (u * s * _silu(z)).astype(_BF16)

    o_ref[...] = x + _dot(y_ref[...], wout_ref[...])


def _gmlp_layer(x, g, w_in, ln_g, ln_b, w_s, b_s_t, w_out):
    m, d = x.shape
    e = w_out.shape[0]
    groups, blk = w_s.shape[0], w_s.shape[1]
    tm = GMLP_TILE
    kern = functools.partial(_gmlp_kernel, e=e, groups=groups, blk=blk)
    return pl.pallas_call(
        kern,
        grid=(m // tm,),
        in_specs=[
            pl.BlockSpec((tm, d), lambda i: (i, 0)),
            _const_spec((1, d)),
            _const_spec((d, 3 * e)),
            _const_spec((1, e)),
            _const_spec((1, e)),
            _const_spec((groups, blk, blk)),
            _const_spec((blk, groups)),
            _const_spec((e, d)),
        ],
        out_specs=pl.BlockSpec((tm, d), lambda i: (i, 0)),
        out_shape=jax.ShapeDtypeStruct((m, d), _F32),
        scratch_shapes=[pltpu.VMEM((tm, e), _BF16), pltpu.VMEM((tm, e), _BF16)],
        compiler_params=_params(1),
        name="gmlp_layer",
    )(x, g, w_in, ln_g, ln_b, w_s, b_s_t, w_out)


def _proj_kernel(x_ref, g_ref, w_ref, a_ref, b_ref, *, e, scale_a):
    h = _rmsnorm(x_ref[...], g_ref[...]).astype(_BF16)
    a = _dot(h, w_ref[:, :e])
    if scale_a != 1.0:
        a = a * scale_a
    a_ref[...] = a.astype(_BF16)
    b_ref[...] = _dot(h, w_ref[:, e:]).astype(_BF16)


def _norm_proj(x, g, w, scale_a=1.0):
    m, d = x.shape
    e = w.shape[1] // 2
    tm = PROJ_TILE
    kern = functools.partial(_proj_kernel, e=e, scale_a=scale_a)
    out = jax.ShapeDtypeStruct((m, e), _BF16)
    return pl.pallas_call(
        kern,
        grid=(m // tm,),
        in_specs=[
            pl.BlockSpec((tm, d), lambda i: (i, 0)),
            _const_spec((1, d)),
            _const_spec((d, 2 * e)),
        ],
        out_specs=[pl.BlockSpec((tm, e), lambda i: (i, 0))] * 2,
        out_shape=[out, out],
        compiler_params=_params(1),
        name="norm_proj",
    )(x, g, w)


def _att_logits(q, kw):
    return lax.dot_general(q, kw, (((1,), (1,)), ((), ())),
                           preferred_element_type=_F32)


def _att_weights(logit, u, carry):
    sp = jnp.log(1.0 + jnp.exp2(jnp.abs(logit) * (-LOG2_E)))
    log_beta = jnp.minimum(logit, 0.0) - sp
    log_1m = log_beta - logit
    after = _dot(log_1m.astype(_BF16), u)
    if carry is not None:
        after = after + carry
    att = jnp.exp(log_beta + after).astype(_BF16)
    return att, after[:, 0:1] + log_1m[:, 0:1]


def _att_kernel(q_ref, k_ref, v_ref, z_ref, y_ref, u_ref, ud_ref, bias_ref, carry_ref, acc_ref,
                *head_refs, blk, dh):
    s_len = q_ref.shape[0]
    n_blk = s_len // blk
    unit = blk // 2
    heads = range(q_ref.shape[1] // dh)
    lgd_ref, lgp_ref, attd_ref, attp_ref = (head_refs[n::4] for n in range(4))

    def suffix_sum_matrix(keys):
        row = lax.broadcasted_iota(jnp.int32, (keys, keys), 0)
        col = lax.broadcasted_iota(jnp.int32, (keys, keys), 1)
        return (row > col).astype(_BF16)

    u_ref[...] = suffix_sum_matrix(blk)
    ud_ref[...] = suffix_sum_matrix(unit)
    row = lax.broadcasted_iota(jnp.int32, (unit, unit), 0)
    col = lax.broadcasted_iota(jnp.int32, (unit, unit), 1)
    bias_ref[...] = jnp.where(col < row, 0.0, MASKED_LOGIT)

    def tile(ref, start, j, rows):
        return ref[pl.ds(start, rows), j * dh:(j + 1) * dh]

    def unit_rows(t0, n):
        return pl.multiple_of(t0 + n * unit, unit)

    def prev_start(r0):
        return pl.multiple_of(jnp.maximum(r0 - blk, 0), unit)

    def gate_store(t0, j, acc):
        z = tile(z_ref, t0, j, blk).astype(_F32)
        y_ref[pl.ds(t0, blk), j * dh:(j + 1) * dh] = (acc * _silu(z)).astype(_BF16)

    def store_logits(t0, j):
        for n in range(2):
            r0 = unit_rows(t0, n)
            q = tile(q_ref, r0, j, unit)
            lgd_ref[j][n] = _att_logits(q, tile(k_ref, r0, j, unit))
            lgp_ref[j][n] = _att_logits(q, tile(k_ref, prev_start(r0), j, blk))

    def store_weights(j):
        carries = []
        for n in range(2):
            att, carry = _att_weights(lgd_ref[j][n] + bias_ref[...], ud_ref[...], None)
            attd_ref[j][n] = att
            attp_ref[j][n], carry = _att_weights(lgp_ref[j][n], u_ref[...], carry)
            carries.append(carry)
        return jnp.concatenate(carries, axis=0)

    def store_first_weights(j):
        carries = []
        for n in range(2):
            attd_ref[j][n], carry = _att_weights(lgd_ref[j][n] + bias_ref[...], ud_ref[...], None)
            carries.append(carry)
        att, _ = _att_weights(lgp_ref[j][1, :, :unit], ud_ref[...], carries[1])
        attp_ref[j][0] = jnp.zeros((unit, blk), _BF16)
        attp_ref[j][1] = jnp.concatenate([att, jnp.zeros((unit, unit), _BF16)], axis=1)

    def finish_block(t0, j):
        accs = []
        for n in range(2):
            r0 = unit_rows(t0, n)
            accs.append(_dot(attd_ref[j][n], tile(v_ref, r0, j, unit))
                        + _dot(attp_ref[j][n], tile(v_ref, prev_start(r0), j, blk)))
        acc = jnp.concatenate(accs, axis=0)
        acc_ref[j] = acc
        gate_store(t0, j, acc)

    def walk_back(t0, slot, worst):
        @pl.when(jnp.logical_and(worst > ATT_DEAD_LOG, t0 - unit > 0))
        def _():
            r1 = unit_rows(t0, 1)
            start1 = pl.multiple_of(t0 - blk, unit)
            new = []
            for j in heads:
                logit = _att_logits(tile(q_ref, r1, j, unit), tile(k_ref, start1, j, unit))
                att, carry = _att_weights(logit, ud_ref[...], carry_ref[slot, j, unit:, :])
                acc_ref[j, unit:, :] += _dot(att, tile(v_ref, start1, j, unit))
                carry_ref[slot, j, unit:, :] = carry
                new.append(jnp.maximum(jnp.max(carry), jnp.max(carry_ref[slot, j, :unit, :])))

            def live(st):
                start, w = st
                return jnp.logical_and(start > 0, w > ATT_DEAD_LOG)

            def window(st):
                start = pl.multiple_of(st[0] - blk, blk)
                new = []
                for j in heads:
                    logit = _att_logits(tile(q_ref, t0, j, blk), tile(k_ref, start, j, blk))
                    att, carry = _att_weights(logit, u_ref[...], carry_ref[slot, j])
                    acc_ref[j] += _dot(att, tile(v_ref, start, j, blk))
                    carry_ref[slot, j] = carry
                    new.append(jnp.max(carry))
                return start, functools.reduce(jnp.maximum, new)

            lax.while_loop(live, window, (start1, functools.reduce(jnp.maximum, new)))
            for j in heads:
                gate_store(t0, j, acc_ref[j])

    for j in heads:
        store_logits(0, j)
    for j in heads:
        store_first_weights(j)
    for j in heads:
        store_logits(blk, j)

    def step(i, worst_prev):
        t0 = pl.multiple_of(i * blk, blk)
        tp = pl.multiple_of(t0 - blk, blk)
        tn = pl.multiple_of(jnp.minimum(t0 + blk, s_len - blk), blk)
        slot = jnp.bitwise_and(i, 1)
        carries = []
        for j in heads:
            finish_block(tp, j)
            carry = store_weights(j)
            carry_ref[slot, j] = carry
            carries.append(carry)
            store_logits(tn, j)
        walk_back(tp, 1 - slot, worst_prev)
        return jnp.max(functools.reduce(jnp.maximum, carries))

    worst = lax.fori_loop(1, n_blk, step, jnp.float32(2.0 * ATT_DEAD_LOG))
    t_last = (n_blk - 1) * blk
    for j in heads:
        finish_block(t_last, j)
    walk_back(t_last, (n_blk - 1) % 2, worst)


def _stick_breaking(q, k, v, z, batch, heads):
    m, e = q.shape
    s_len = m // batch
    dh = e // heads
    blk, nh = ATT_BLOCK, ATT_HEADS_PER_STEP
    unit = blk // 2
    spec = pl.BlockSpec((s_len, nh * dh), lambda b, h: (b, h))
    per_head = [pltpu.VMEM((2, unit, unit), _F32),
                pltpu.VMEM((2, unit, blk), _F32),
                pltpu.VMEM((2, unit, unit), _BF16),
                pltpu.VMEM((2, unit, blk), _BF16)]
    return pl.pallas_call(
        functools.partial(_att_kernel, blk=blk, dh=dh),
        grid=(batch, heads // nh),
        in_specs=[spec] * 4,
        out_specs=spec,
        out_shape=jax.ShapeDtypeStruct((m, e), _BF16),
        scratch_shapes=[pltpu.VMEM((blk, blk), _BF16),
                        pltpu.VMEM((unit, unit), _BF16),
                        pltpu.VMEM((unit, unit), _F32),
                        pltpu.VMEM((2, nh, blk, 1), _F32),
                        pltpu.VMEM((nh, blk, dh), _F32)] + per_head * nh,
        compiler_params=_params(2),
        name="stick_breaking",
    )(q, k, v, z)


def _out_final_kernel(x_ref, y_ref, w_ref, gf_ref, o_ref):
    o_ref[...] = _rmsnorm(x_ref[...] + _dot(y_ref[...], w_ref[...]), gf_ref[...])


def _out_final(x, y, w, final_g):
    m, d = x.shape
    e = y.shape[1]
    tm = PROJ_TILE
    return pl.pallas_call(
        _out_final_kernel,
        grid=(m // tm,),
        in_specs=[pl.BlockSpec((tm, d), lambda i: (i, 0)),
                  pl.BlockSpec((tm, e), lambda i: (i, 0)),
                  _const_spec((e, d)),
                  _const_spec((1, d))],
        out_specs=pl.BlockSpec((tm, d), lambda i: (i, 0)),
        out_shape=jax.ShapeDtypeStruct((m, d), _F32),
        compiler_params=_params(1),
        name="out_final",
    )(x, y, w, final_g)


def _out_next_kernel(x_ref, y_ref, wout_ref, g_ref, win_ref, o_ref, a_ref, b_ref, *, e, scale_a):
    half = x_ref.shape[0] // 2
    new = [x_ref[r:r + half, :] + _dot(y_ref[r:r + half, :], wout_ref[...])
           for r in (0, half)]
    for n, r in enumerate((0, half)):
        o_ref[r:r + half, :] = new[n]
        h = _rmsnorm(new[n], g_ref[...]).astype(_BF16)
        a_ref[r:r + half, :] = (_dot(h, win_ref[:, :e]) * scale_a).astype(_BF16)
        b_ref[r:r + half, :] = _dot(h, win_ref[:, e:]).astype(_BF16)


def _out_next(x, y, w_out, g, w_in, scale_a):
    m, d = x.shape
    e = y.shape[1]
    tm = PROJ_TILE
    half_out = jax.ShapeDtypeStruct((m, e), _BF16)
    return pl.pallas_call(
        functools.partial(_out_next_kernel, e=e, scale_a=scale_a),
        grid=(m // tm,),
        in_specs=[pl.BlockSpec((tm, d), lambda i: (i, 0)),
                  pl.BlockSpec((tm, e), lambda i: (i, 0)),
                  _const_spec((e, d)),
                  _const_spec((1, d)),
                  _const_spec((d, 2 * e))],
        out_specs=[pl.BlockSpec((tm, d), lambda i: (i, 0)),
                   pl.BlockSpec((tm, e), lambda i: (i, 0)),
                   pl.BlockSpec((tm, e), lambda i: (i, 0))],
        out_shape=[jax.ShapeDtypeStruct((m, d), _F32), half_out, half_out],
        compiler_params=_params(1),
        name="out_next",
    )(x, y, w_out, g, w_in)


def kernel(x, a_norm, a_w_in, a_ln_g, a_ln_b, a_w_s, a_b_s, a_w_out, kv_norm, w_kv,
           b_norm, b_w_in, b_w_out, final_norm):
    batch, s_len, d = x.shape
    e = w_kv.shape[1] // 2
    n_a, n_b = a_w_in.shape[0], b_w_in.shape[0]
    assert n_b >= 1 and a_w_s.shape[1:] == (GMLP_GROUPS, GMLP_BLOCK, GMLP_BLOCK)
    assert (batch * s_len) % PROJ_TILE == 0 and s_len % ATT_BLOCK == 0 and s_len >= 2 * ATT_BLOCK
    assert GMLP_TILE % GMLP_BLOCK == 0 and SB_HEADS % ATT_HEADS_PER_STEP == 0

    xs = x.reshape(batch * s_len, d)
    for i in range(n_a):
        xs = _gmlp_layer(xs, a_norm[i][None], a_w_in[i].astype(_BF16), a_ln_g[i][None],
                         a_ln_b[i][None], a_w_s[i], a_b_s[i].T, a_w_out[i].astype(_BF16))

    k, v = _norm_proj(xs, kv_norm[None], w_kv.astype(_BF16))
    scale = 1.0 / math.sqrt(e // SB_HEADS)
    q, z = _norm_proj(xs, b_norm[0][None], b_w_in[0].astype(_BF16), scale_a=scale)
    for i in range(n_b):
        y = _stick_breaking(q, k, v, z, batch, SB_HEADS)
        w_out = b_w_out[i].astype(_BF16)
        if i + 1 < n_b:
            xs, q, z = _out_next(xs, y, w_out, b_norm[i + 1][None],
                                 b_w_in[i + 1].astype(_BF16), scale)
        else:
            xs = _out_final(xs, y, w_out, final_norm[None])
    return xs.reshape(batch, s_len, d)
```

```python
import functools
import math

import jax
import jax.numpy as jnp
from jax import lax
from jax.experimental import pallas as pl
from jax.experimental.pallas import tpu as pltpu

EPS = 1e-6
CHUNK = 64
GMLP_BLOCK = 128
GMLP_GROUPS = 8
SB_HEADS = 16

VMEM_LIMIT_BYTES_V7X = 56 * 1024 * 1024

GMLP_TILE = 256
PROJ_TILE = 512
ATT_BLOCK = 256
ATT_HEADS_PER_STEP = 4

ATT_DEAD_LOG = -105.0
MASKED_LOGIT = -1e30
LOG2_E = math.log2(math.e)

_BF16 = jnp.bfloat16
_F32 = jnp.float32


def _dot(a, b):
    return jnp.dot(a, b, preferred_element_type=_F32)


def _gelu(x):
    c = math.sqrt(2.0 / math.pi)
    hx = 0.5 * x
    return hx + hx * jnp.tanh(x * (c + (c * 0.044715) * (x * x)))


def _silu(z):
    hz = 0.5 * z
    return hz + hz * jnp.tanh(hz)


def _rmsnorm(x, g):
    return x * lax.rsqrt(jnp.mean(x * x, axis=-1, keepdims=True) + EPS) * g


def _const_spec(shape):
    nd = len(shape)
    return pl.BlockSpec(shape, lambda *_: (0,) * nd, pipeline_mode=pl.Buffered(1))


def _params(n_axes):
    return pltpu.CompilerParams(
        dimension_semantics=("arbitrary",) * n_axes,
        vmem_limit_bytes=VMEM_LIMIT_BYTES_V7X)


def _gmlp_kernel(x_ref, g_ref, win_ref, lng_ref, lnb_ref, ws_ref, bs_ref, wout_ref,
                 o_ref, vn_ref, y_ref, *, e, groups, blk):
    tm = x_ref.shape[0]
    gd = e // groups
    x = x_ref[...]
    h = _rmsnorm(x, g_ref[...]).astype(_BF16)

    v = _gelu(_dot(h, win_ref[:, e:2 * e]))
    mu = jnp.mean(v, axis=-1, keepdims=True)
    vc = v - mu
    var = jnp.mean(vc * vc, axis=-1, keepdims=True)
    vn = vc * lax.rsqrt(var + EPS) * lng_ref[...] + lnb_ref[...]
    vn_ref[...] = vn.astype(_BF16)

    r = lax.broadcasted_iota(jnp.int32, (blk, blk), 0) // CHUNK
    c = lax.broadcasted_iota(jnp.int32, (blk, blk), 1) // CHUNK
    mask = (r >= c).astype(_F32)

    for g in range(groups):
        lo, hi = g * gd, (g + 1) * gd
        u = _gelu(_dot(h, win_ref[:, lo:hi]))
        z = _dot(h, win_ref[:, 2 * e + lo:2 * e + hi])
        wm = (ws_ref[g] * mask).astype(_BF16)
        bcol = bs_ref[:, g:g + 1]
        s = jnp.concatenate(
            [_dot(wm, vn_ref[b * blk:(b + 1) * blk, lo:hi]) + bcol
             for b in range(tm // blk)], axis=0)
        y_ref[:, lo:hi] = (u * s * _silu(z)).astype(_BF16)

    o_ref[...] = x + _dot(y_ref[...], wout_ref[...])


def _gmlp_layer(x, g, w_in, ln_g, ln_b, w_s, b_s_t, w_out):
    m, d = x.shape
    e = w_out.shape[0]
    groups, blk = w_s.shape[0], w_s.shape[1]
    tm = GMLP_TILE
    kern = functools.partial(_gmlp_kernel, e=e, groups=groups, blk=blk)
    return pl.pallas_call(
        kern,
        grid=(m // tm,),
        in_specs=[
            pl.BlockSpec((tm, d), lambda i: (i, 0)),
            _const_spec((1, d)),
            _const_spec((d, 3 * e)),
            _const_spec((1, e)),
            _const_spec((1, e)),
            _const_spec((groups, blk, blk)),
            _const_spec((blk, groups)),
            _const_spec((e, d)),
        ],
        out_specs=pl.BlockSpec((tm, d), lambda i: (i, 0)),
        out_shape=jax.ShapeDtypeStruct((m, d), _F32),
        scratch_shapes=[pltpu.VMEM((tm, e), _BF16), pltpu.VMEM((tm, e), _BF16)],
        compiler_params=_params(1),
        name="gmlp_layer",
    )(x, g, w_in, ln_g, ln_b, w_s, b_s_t, w_out)


def _proj_kernel(x_ref, g_ref, w_ref, a_ref, b_ref, *, e, scale_a):
    h = _rmsnorm(x_ref[...], g_ref[...]).astype(_BF16)
    a = _dot(h, w_ref[:, :e])
    if scale_a != 1.0:
        a = a * scale_a
    a_ref[...] = a.astype(_BF16)
    b_ref[...] = _dot(h, w_ref[:, e:]).astype(_BF16)


def _norm_proj(x, g, w, scale_a=1.0):
    m, d = x.shape
    e = w.shape[1] // 2
    tm = PROJ_TILE
    kern = functools.partial(_proj_kernel, e=e, scale_a=scale_a)
    out = jax.ShapeDtypeStruct((m, e), _BF16)
    return pl.pallas_call(
        kern,
        grid=(m // tm,),
        in_specs=[
            pl.BlockSpec((tm, d), lambda i: (i, 0)),
            _const_spec((1, d)),
            _const_spec((d, 2 * e)),
        ],
        out_specs=[pl.BlockSpec((tm, e), lambda i: (i, 0))] * 2,
        out_shape=[out, out],
        compiler_params=_params(1),
        name="norm_proj",
    )(x, g, w)


def _att_logits(q, kw):
    return lax.dot_general(q, kw, (((1,), (1,)), ((), ())),
                           preferred_element_type=_F32)


def _att_weights(logit, u, carry):
    sp = jnp.log(1.0 + jnp.exp2(jnp.abs(logit) * (-LOG2_E)))
    log_beta = jnp.minimum(logit, 0.0) - sp
    log_1m = log_beta - logit
    after = _dot(log_1m.astype(_BF16), u)
    if carry is not None:
        after = after + carry
    att = jnp.exp(log_beta + after).astype(_BF16)
    return att, after[:, 0:1] + log_1m[:, 0:1]


def _att_kernel(q_ref, k_ref, v_ref, z_ref, y_ref, u_ref, ud_ref, bias_ref, carry_ref, acc_ref,
                worst_ref, *head_refs, blk, dh):
    s_len = q_ref.shape[0]
    n_blk = s_len // blk
    unit = blk // 2
    heads = range(q_ref.shape[1] // dh)
    lgd_ref, lgp_ref, attd_ref, attp_ref = (head_refs[n::4] for n in range(4))

    def suffix_sum_matrix(keys):
        row = lax.broadcasted_iota(jnp.int32, (keys, keys), 0)
        col = lax.broadcasted_iota(jnp.int32, (keys, keys), 1)
        return (row > col).astype(_BF16)

    u_ref[...] = suffix_sum_matrix(blk)
    ud_ref[...] = suffix_sum_matrix(unit)
    row = lax.broadcasted_iota(jnp.int32, (unit, unit), 0)
    col = lax.broadcasted_iota(jnp.int32, (unit, unit), 1)
    bias_ref[...] = jnp.where(col < row, 0.0, MASKED_LOGIT)

    def tile(ref, start, j, rows):
        return ref[pl.ds(start, rows), j * dh:(j + 1) * dh]

    def unit_rows(t0, n):
        return pl.multiple_of(t0 + n * unit, unit)

    def prev_start(r0):
        return pl.multiple_of(jnp.maximum(r0 - blk, 0), unit)

    def store_logits(t0, j):
        for n in range(2):
            r0 = unit_rows(t0, n)
            q = tile(q_ref, r0, j, unit)
            lgd_ref[j][n] = _att_logits(q, tile(k_ref, r0, j, unit))
            lgp_ref[j][n] = _att_logits(q, tile(k_ref, prev_start(r0), j, blk))

    def store_weights(j):
        carries = []
        for n in range(2):
            att, carry = _att_weights(lgd_ref[j][n] + bias_ref[...], ud_ref[...], None)
            attd_ref[j][n] = att
            attp_ref[j][n], carry = _att_weights(lgp_ref[j][n], u_ref[...], carry)
            carries.append(carry)
        return jnp.concatenate(carries, axis=0)

    def store_first_weights(j):
        carries = []
        for n in range(2):
            attd_ref[j][n], carry = _att_weights(lgd_ref[j][n] + bias_ref[...], ud_ref[...], None)
            carries.append(carry)
        att, _ = _att_weights(lgp_ref[j][1, :, :unit], ud_ref[...], carries[1])
        attp_ref[j][0] = jnp.zeros((unit, blk), _BF16)
        attp_ref[j][1] = jnp.concatenate([att, jnp.zeros((unit, unit), _BF16)], axis=1)

    def finish_block(t0, j):
        accs = []
        for n in range(2):
            r0 = unit_rows(t0, n)
            accs.append(_dot(attd_ref[j][n], tile(v_ref, r0, j, unit))
                        + _dot(attp_ref[j][n], tile(v_ref, prev_start(r0), j, blk)))
        z = tile(z_ref, t0, j, blk).astype(_F32)
        y_ref[pl.ds(t0, blk), j * dh:(j + 1) * dh] = (
            jnp.concatenate(accs, axis=0) * _silu(z)).astype(_BF16)

    def step(i):
        t0 = pl.multiple_of(i * blk, blk)
        tp = pl.multiple_of(t0 - blk, blk)
        tn = pl.multiple_of(jnp.minimum(t0 + blk, s_len - blk), blk)
        carries = []
        for j in heads:
            finish_block(tp, j)
            carries.append(store_weights(j))
            store_logits(tn, j)
        worst_ref[i] = jnp.max(functools.reduce(jnp.maximum, carries))

    def redo_unit(r0):
        worst = []
        for j in heads:
            logit = _att_logits(tile(q_ref, r0, j, unit), tile(k_ref, r0, j, unit))
            att, carry = _att_weights(logit + bias_ref[...], ud_ref[...], None)
            acc_ref[j] = _dot(att, tile(v_ref, r0, j, unit))
            carry_ref[j] = carry
            worst.append(jnp.max(carry))

        def live(st):
            start, w = st
            return jnp.logical_and(start > 0, w > ATT_DEAD_LOG)

        def window(st):
            start = pl.multiple_of(st[0] - unit, unit)
            worst = []
            for j in heads:
                logit = _att_logits(tile(q_ref, r0, j, unit), tile(k_ref, start, j, unit))
                att, carry = _att_weights(logit, ud_ref[...], carry_ref[j])
                acc_ref[j] += _dot(att, tile(v_ref, start, j, unit))
                carry_ref[j] = carry
                worst.append(jnp.max(carry))
            return start, functools.reduce(jnp.maximum, worst)

        lax.while_loop(live, window, (r0, functools.reduce(jnp.maximum, worst)))
        for j in heads:
            z = tile(z_ref, r0, j, unit).astype(_F32)
            y_ref[pl.ds(r0, unit), j * dh:(j + 1) * dh] = (acc_ref[j] * _silu(z)).astype(_BF16)

    for j in heads:
        store_logits(0, j)
    for j in heads:
        store_first_weights(j)
    for j in heads:
        store_logits(blk, j)

    def two_steps(p, c):
        step(2 * p + 1)
        step(2 * p + 2)
        return c

    n_pairs = (n_blk - 1) // 2
    lax.fori_loop(0, n_pairs, two_steps, 0)
    for i in range(2 * n_pairs + 1, n_blk):
        step(jnp.int32(i))
    for j in heads:
        finish_block((n_blk - 1) * blk, j)

    def redo_if_live(i, c):
        @pl.when(worst_ref[i] > ATT_DEAD_LOG)
        def _():
            redo_unit(pl.multiple_of(i * blk, blk))
            redo_unit(pl.multiple_of(i * blk + unit, unit))
        return c

    lax.fori_loop(1, n_blk, redo_if_live, 0)


def _stick_breaking(q, k, v, z, batch, heads):
    m, e = q.shape
    s_len = m // batch
    dh = e // heads
    blk, nh = ATT_BLOCK, ATT_HEADS_PER_STEP
    unit = blk // 2
    spec = pl.BlockSpec((s_len, nh * dh), lambda b, h: (b, h))
    per_head = [pltpu.VMEM((2, unit, unit), _F32),
                pltpu.VMEM((2, unit, blk), _F32),
                pltpu.VMEM((2, unit, unit), _BF16),
                pltpu.VMEM((2, unit, blk), _BF16)]
    return pl.pallas_call(
        functools.partial(_att_kernel, blk=blk, dh=dh),
        grid=(batch, heads // nh),
        in_specs=[spec] * 4,
        out_specs=spec,
        out_shape=jax.ShapeDtypeStruct((m, e), _BF16),
        scratch_shapes=[pltpu.VMEM((blk, blk), _BF16),
                        pltpu.VMEM((unit, unit), _BF16),
                        pltpu.VMEM((unit, unit), _F32),
                        pltpu.VMEM((nh, unit, 1), _F32),
                        pltpu.VMEM((nh, unit, dh), _F32),
                        pltpu.SMEM((s_len // blk,), _F32)] + per_head * nh,
        compiler_params=_params(2),
        name="stick_breaking",
    )(q, k, v, z)


def _out_final_kernel(x_ref, y_ref, w_ref, gf_ref, o_ref):
    o_ref[...] = _rmsnorm(x_ref[...] + _dot(y_ref[...], w_ref[...]), gf_ref[...])


def _out_final(x, y, w, final_g):
    m, d = x.shape
    e = y.shape[1]
    tm = PROJ_TILE
    return pl.pallas_call(
        _out_final_kernel,
        grid=(m // tm,),
        in_specs=[pl.BlockSpec((tm, d), lambda i: (i, 0)),
                  pl.BlockSpec((tm, e), lambda i: (i, 0)),
                  _const_spec((e, d)),
                  _const_spec((1, d))],
        out_specs=pl.BlockSpec((tm, d), lambda i: (i, 0)),
        out_shape=jax.ShapeDtypeStruct((m, d), _F32),
        compiler_params=_params(1),
        name="out_final",
    )(x, y, w, final_g)


def _out_next_kernel(x_ref, y_ref, wout_ref, g_ref, win_ref, o_ref, a_ref, b_ref, *, e, scale_a):
    half = x_ref.shape[0] // 2
    new = [x_ref[r:r + half, :] + _dot(y_ref[r:r + half, :], wout_ref[...])
           for r in (0, half)]
    for n, r in enumerate((0, half)):
        o_ref[r:r + half, :] = new[n]
        h = _rmsnorm(new[n], g_ref[...]).astype(_BF16)
        a_ref[r:r + half, :] = (_dot(h, win_ref[:, :e]) * scale_a).astype(_BF16)
        b_ref[r:r + half, :] = _dot(h, win_ref[:, e:]).astype(_BF16)


def _out_next(x, y, w_out, g, w_in, scale_a):
    m, d = x.shape
    e = y.shape[1]
    tm = PROJ_TILE
    half_out = jax.ShapeDtypeStruct((m, e), _BF16)
    return pl.pallas_call(
        functools.partial(_out_next_kernel, e=e, scale_a=scale_a),
        grid=(m // tm,),
        in_specs=[pl.BlockSpec((tm, d), lambda i: (i, 0)),
                  pl.BlockSpec((tm, e), lambda i: (i, 0)),
                  _const_spec((e, d)),
                  _const_spec((1, d)),
                  _const_spec((d, 2 * e))],
        out_specs=[pl.BlockSpec((tm, d), lambda i: (i, 0)),
                   pl.BlockSpec((tm, e), lambda i: (i, 0)),
                   pl.BlockSpec((tm, e), lambda i: (i, 0))],
        out_shape=[jax.ShapeDtypeStruct((m, d), _F32), half_out, half_out],
        compiler_params=_params(1),
        name="out_next",
    )(x, y, w_out, g, w_in)


def kernel(x, a_norm, a_w_in, a_ln_g, a_ln_b, a_w_s, a_b_s, a_w_out, kv_norm, w_kv,
           b_norm, b_w_in, b_w_out, final_norm):
    batch, s_len, d = x.shape
    e = w_kv.shape[1] // 2
    n_a, n_b = a_w_in.shape[0], b_w_in.shape[0]
    assert n_b >= 1 and a_w_s.shape[1:] == (GMLP_GROUPS, GMLP_BLOCK, GMLP_BLOCK)
    assert (batch * s_len) % PROJ_TILE == 0 and s_len % ATT_BLOCK == 0 and s_len >= 2 * ATT_BLOCK
    assert GMLP_TILE % GMLP_BLOCK == 0 and SB_HEADS % ATT_HEADS_PER_STEP == 0

    xs = x.reshape(batch * s_len, d)
    for i in range(n_a):
        xs = _gmlp_layer(xs, a_norm[i][None], a_w_in[i].astype(_BF16), a_ln_g[i][None],
                         a_ln_b[i][None], a_w_s[i], a_b_s[i].T, a_w_out[i].astype(_BF16))

    k, v = _norm_proj(xs, kv_norm[None], w_kv.astype(_BF16))
    scale = 1.0 / math.sqrt(e // SB_HEADS)
    q, z = _norm_proj(xs, b_norm[0][None], b_w_in[0].astype(_BF16), scale_a=scale)
    for i in range(n_b):
        y = _stick_breaking(q, k, v, z, batch, SB_HEADS)
        w_out = b_w_out[i].astype(_BF16)
        if i + 1 < n_b:
            xs, q, z = _out_next(xs, y, w_out, b_norm[i + 1][None],
                                 b_w_in[i + 1].astype(_BF16), scale)
        else:
            xs = _out_final(xs, y, w_out, final_norm[None])
    return xs.reshape(batch, s_len, d)
```

```python
import functools
import math

import jax
import jax.numpy as jnp
from jax import lax
from jax.experimental import pallas as pl
from jax.experimental.pallas import tpu as pltpu

EPS = 1e-6
CHUNK = 64
GMLP_BLOCK = 128
GMLP_GROUPS = 8
SB_HEADS = 16

VMEM_LIMIT_BYTES_V7X = 56 * 1024 * 1024

GMLP_TILE = 512
PROJ_TILE = 512
ATT_BLOCK = 256
ATT_HEADS_PER_STEP = 4
ATT_BLOCKS_PER_ITER = 5

ATT_DEAD_LOG = -105.0
MASKED_LOGIT = -1e30
LOG2_E = math.log2(math.e)

_BF16 = jnp.bfloat16
_F32 = jnp.float32


def _dot(a, b):
    return jnp.dot(a, b, preferred_element_type=_F32)


def _gelu(x):
    c = math.sqrt(2.0 / math.pi)
    hx = 0.5 * x
    return hx + hx * jnp.tanh(x * (c + (c * 0.044715) * (x * x)))


def _silu(z):
    hz = 0.5 * z
    return hz + hz * jnp.tanh(hz)


def _rmsnorm(x, g):
    return x * lax.rsqrt(jnp.mean(x * x, axis=-1, keepdims=True) + EPS) * g


def _const_spec(shape):
    nd = len(shape)
    return pl.BlockSpec(shape, lambda *_: (0,) * nd, pipeline_mode=pl.Buffered(1))


def _params(n_axes):
    return pltpu.CompilerParams(
        dimension_semantics=("arbitrary",) * n_axes,
        vmem_limit_bytes=VMEM_LIMIT_BYTES_V7X)


def _gmlp_kernel(x_ref, g_ref, win_ref, lng_ref, lnb_ref, ws_ref, bs_ref, wout_ref,
                 o_ref, vn_ref, y_ref, *, e, groups, blk):
    tm = x_ref.shape[0]
    gd = e // groups
    x = x_ref[...]
    h = _rmsnorm(x, g_ref[...]).astype(_BF16)

    v = _gelu(_dot(h, win_ref[:, e:2 * e]))
    mu = jnp.mean(v, axis=-1, keepdims=True)
    vc = v - mu
    var = jnp.mean(vc * vc, axis=-1, keepdims=True)
    vn = vc * lax.rsqrt(var + EPS) * lng_ref[...] + lnb_ref[...]
    vn_ref[...] = vn.astype(_BF16)

    r = lax.broadcasted_iota(jnp.int32, (blk, blk), 0) // CHUNK
    c = lax.broadcasted_iota(jnp.int32, (blk, blk), 1) // CHUNK
    mask = (r >= c).astype(_F32)

    for g in range(groups):
        lo, hi = g * gd, (g + 1) * gd
        u = _gelu(_dot(h, win_ref[:, lo:hi]))
        z = _dot(h, win_ref[:, 2 * e + lo:2 * e + hi])
        wm = (ws_ref[g] * mask).astype(_BF16)
        bcol = bs_ref[:, g:g + 1]
        s = jnp.concatenate(
            [_dot(wm, vn_ref[b * blk:(b + 1) * blk, lo:hi]) + bcol
             for b in range(tm // blk)], axis=0)
        y_ref[:, lo:hi] = (u * s * _silu(z)).astype(_BF16)

    o_ref[...] = x + _dot(y_ref[...], wout_ref[...])


def _gmlp_layer(x, g, w_in, ln_g, ln_b, w_s, b_s_t, w_out):
    m, d = x.shape
    e = w_out.shape[0]
    groups, blk = w_s.shape[0], w_s.shape[1]
    tm = GMLP_TILE
    kern = functools.partial(_gmlp_kernel, e=e, groups=groups, blk=blk)
    return pl.pallas_call(
        kern,
        grid=(m // tm,),
        in_specs=[
            pl.BlockSpec((tm, d), lambda i: (i, 0)),
            _const_spec((1, d)),
            _const_spec((d, 3 * e)),
            _const_spec((1, e)),
            _const_spec((1, e)),
            _const_spec((groups, blk, blk)),
            _const_spec((blk, groups)),
            _const_spec((e, d)),
        ],
        out_specs=pl.BlockSpec((tm, d), lambda i: (i, 0)),
        out_shape=jax.ShapeDtypeStruct((m, d), _F32),
        scratch_shapes=[pltpu.VMEM((tm, e), _BF16), pltpu.VMEM((tm, e), _BF16)],
        compiler_params=_params(1),
        name="gmlp_layer",
    )(x, g, w_in, ln_g, ln_b, w_s, b_s_t, w_out)


def _proj_kernel(x_ref, g_ref, w_ref, a_ref, b_ref, *, e, scale_a):
    h = _rmsnorm(x_ref[...], g_ref[...]).astype(_BF16)
    a = _dot(h, w_ref[:, :e])
    if scale_a != 1.0:
        a = a * scale_a
    a_ref[...] = a.astype(_BF16)
    b_ref[...] = _dot(h, w_ref[:, e:]).astype(_BF16)


def _norm_proj(x, g, w, scale_a=1.0):
    m, d = x.shape
    e = w.shape[1] // 2
    tm = PROJ_TILE
    kern = functools.partial(_proj_kernel, e=e, scale_a=scale_a)
    out = jax.ShapeDtypeStruct((m, e), _BF16)
    return pl.pallas_call(
        kern,
        grid=(m // tm,),
        in_specs=[
            pl.BlockSpec((tm, d), lambda i: (i, 0)),
            _const_spec((1, d)),
            _const_spec((d, 2 * e)),
        ],
        out_specs=[pl.BlockSpec((tm, e), lambda i: (i, 0))] * 2,
        out_shape=[out, out],
        compiler_params=_params(1),
        name="norm_proj",
    )(x, g, w)


def _att_logits(q, kw):
    return lax.dot_general(q, kw, (((1,), (1,)), ((), ())),
                           preferred_element_type=_F32)


def _att_weights(logit, u, carry):
    sp = jnp.log(1.0 + jnp.exp2(jnp.abs(logit) * (-LOG2_E)))
    log_beta = jnp.minimum(logit, 0.0) - sp
    log_1m = log_beta - logit
    after = _dot(log_1m.astype(_BF16), u)
    if carry is not None:
        after = after + carry
    att = jnp.exp(log_beta + after).astype(_BF16)
    return att, after[:, 0:1] + log_1m[:, 0:1]


def _att_kernel(q_ref, k_ref, v_ref, z_ref, y_ref, u_ref, ud_ref, bias_ref, carry_ref, acc_ref,
                worst_ref, *head_refs, blk, dh):
    s_len = q_ref.shape[0]
    n_blk = s_len // blk
    unit = blk // 2
    heads = range(q_ref.shape[1] // dh)
    lgd_ref, lgp_ref, attd_ref, attp_ref = (head_refs[n::4] for n in range(4))

    def suffix_sum_matrix(keys):
        row = lax.broadcasted_iota(jnp.int32, (keys, keys), 0)
        col = lax.broadcasted_iota(jnp.int32, (keys, keys), 1)
        return (row > col).astype(_BF16)

    u_ref[...] = suffix_sum_matrix(blk)
    ud_ref[...] = suffix_sum_matrix(unit)
    row = lax.broadcasted_iota(jnp.int32, (unit, unit), 0)
    col = lax.broadcasted_iota(jnp.int32, (unit, unit), 1)
    bias_ref[...] = jnp.where(col < row, 0.0, MASKED_LOGIT)

    def tile(ref, start, j, rows):
        return ref[pl.ds(start, rows), j * dh:(j + 1) * dh]

    def unit_rows(t0, n):
        return pl.multiple_of(t0 + n * unit, unit)

    def prev_start(r0):
        return pl.multiple_of(jnp.maximum(r0 - blk, 0), unit)

    def store_logits(t0, j):
        for n in range(2):
            r0 = unit_rows(t0, n)
            q = tile(q_ref, r0, j, unit)
            lgd_ref[j][n] = _att_logits(q, tile(k_ref, r0, j, unit))
            lgp_ref[j][n] = _att_logits(q, tile(k_ref, prev_start(r0), j, blk))

    def store_weights(j):
        carries = []
        for n in range(2):
            att, carry = _att_weights(lgd_ref[j][n] + bias_ref[...], ud_ref[...], None)
            attd_ref[j][n] = att
            attp_ref[j][n], carry = _att_weights(lgp_ref[j][n], u_ref[...], carry)
            carries.append(carry)
        return jnp.concatenate(carries, axis=0)

    def store_first_weights(j):
        carries = []
        for n in range(2):
            attd_ref[j][n], carry = _att_weights(lgd_ref[j][n] + bias_ref[...], ud_ref[...], None)
            carries.append(carry)
        att, _ = _att_weights(lgp_ref[j][1, :, :unit], ud_ref[...], carries[1])
        attp_ref[j][0] = jnp.zeros((unit, blk), _BF16)
        attp_ref[j][1] = jnp.concatenate([att, jnp.zeros((unit, unit), _BF16)], axis=1)

    def finish_block(t0, j):
        accs = []
        for n in range(2):
            r0 = unit_rows(t0, n)
            accs.append(_dot(attd_ref[j][n], tile(v_ref, r0, j, unit))
                        + _dot(attp_ref[j][n], tile(v_ref, prev_start(r0), j, blk)))
        z = tile(z_ref, t0, j, blk).astype(_F32)
        y_ref[pl.ds(t0, blk), j * dh:(j + 1) * dh] = (
            jnp.concatenate(accs, axis=0) * _silu(z)).astype(_BF16)

    def step(i):
        t0 = pl.multiple_of(i * blk, blk)
        tp = pl.multiple_of(t0 - blk, blk)
        tn = pl.multiple_of(jnp.minimum(t0 + blk, s_len - blk), blk)
        carries = []
        for j in heads:
            finish_block(tp, j)
            carries.append(store_weights(j))
            store_logits(tn, j)
        worst_ref[i] = jnp.max(functools.reduce(jnp.maximum, carries))

    def redo_unit(r0):
        worst = []
        for j in heads:
            logit = _att_logits(tile(q_ref, r0, j, unit), tile(k_ref, r0, j, unit))
            att, carry = _att_weights(logit + bias_ref[...], ud_ref[...], None)
            acc_ref[j] = _dot(att, tile(v_ref, r0, j, unit))
            carry_ref[j] = carry
            worst.append(jnp.max(carry))

        def live(st):
            start, w = st
            return jnp.logical_and(start > 0, w > ATT_DEAD_LOG)

        def window(st):
            start = pl.multiple_of(st[0] - unit, unit)
            worst = []
            for j in heads:
                logit = _att_logits(tile(q_ref, r0, j, unit), tile(k_ref, start, j, unit))
                att, carry = _att_weights(logit, ud_ref[...], carry_ref[j])
                acc_ref[j] += _dot(att, tile(v_ref, start, j, unit))
                carry_ref[j] = carry
                worst.append(jnp.max(carry))
            return start, functools.reduce(jnp.maximum, worst)

        lax.while_loop(live, window, (r0, functools.reduce(jnp.maximum, worst)))
        for j in heads:
            z = tile(z_ref, r0, j, unit).astype(_F32)
            y_ref[pl.ds(r0, unit), j * dh:(j + 1) * dh] = (acc_ref[j] * _silu(z)).astype(_BF16)

    for j in heads:
        store_logits(0, j)
    for j in heads:
        store_first_weights(j)
    for j in heads:
        store_logits(blk, j)

    per = ATT_BLOCKS_PER_ITER

    def steps(p, c):
        for n in range(per):
            step(per * p + 1 + n)
        return c

    n_iter = (n_blk - 1) // per
    lax.fori_loop(0, n_iter, steps, 0)
    for i in range(per * n_iter + 1, n_blk):
        step(jnp.int32(i))
    for j in heads:
        finish_block((n_blk - 1) * blk, j)

    def redo_if_live(i, c):
        @pl.when(worst_ref[i] > ATT_DEAD_LOG)
        def _():
            redo_unit(pl.multiple_of(i * blk, blk))
            redo_unit(pl.multiple_of(i * blk + unit, unit))
        return c

    lax.fori_loop(1, n_blk, redo_if_live, 0)


def _stick_breaking(q, k, v, z, batch, heads):
    m, e = q.shape
    s_len = m // batch
    dh = e // heads
    blk, nh = ATT_BLOCK, ATT_HEADS_PER_STEP
    unit = blk // 2
    spec = pl.BlockSpec((s_len, nh * dh), lambda b, h: (b, h))
    per_head = [pltpu.VMEM((2, unit, unit), _F32),
                pltpu.VMEM((2, unit, blk), _F32),
                pltpu.VMEM((2, unit, unit), _BF16),
                pltpu.VMEM((2, unit, blk), _BF16)]
    return pl.pallas_call(
        functools.partial(_att_kernel, blk=blk, dh=dh),
        grid=(batch, heads // nh),
        in_specs=[spec] * 4,
        out_specs=spec,
        out_shape=jax.ShapeDtypeStruct((m, e), _BF16),
        scratch_shapes=[pltpu.VMEM((blk, blk), _BF16),
                        pltpu.VMEM((unit, unit), _BF16),
                        pltpu.VMEM((unit, unit), _F32),
                        pltpu.VMEM((nh, unit, 1), _F32),
                        pltpu.VMEM((nh, unit, dh), _F32),
                        pltpu.SMEM((s_len // blk,), _F32)] + per_head * nh,
        compiler_params=_params(2),
        name="stick_breaking",
    )(q, k, v, z)


def _out_final_kernel(x_ref, y_ref, w_ref, gf_ref, o_ref):
    o_ref[...] = _rmsnorm(x_ref[...] + _dot(y_ref[...], w_ref[...]), gf_ref[...])


def _out_final(x, y, w, final_g):
    m, d = x.shape
    e = y.shape[1]
    tm = PROJ_TILE
    return pl.pallas_call(
        _out_final_kernel,
        grid=(m // tm,),
        in_specs=[pl.BlockSpec((tm, d), lambda i: (i, 0)),
                  pl.BlockSpec((tm, e), lambda i: (i, 0)),
                  _const_spec((e, d)),
                  _const_spec((1, d))],
        out_specs=pl.BlockSpec((tm, d), lambda i: (i, 0)),
        out_shape=jax.ShapeDtypeStruct((m, d), _F32),
        compiler_params=_params(1),
        name="out_final",
    )(x, y, w, final_g)


def _out_next_kernel(x_ref, y_ref, wout_ref, g_ref, win_ref, o_ref, a_ref, b_ref, *, e, scale_a):
    half = x_ref.shape[0] // 2
    new = [x_ref[r:r + half, :] + _dot(y_ref[r:r + half, :], wout_ref[...])
           for r in (0, half)]
    for n, r in enumerate((0, half)):
        o_ref[r:r + half, :] = new[n]
        h = _rmsnorm(new[n], g_ref[...]).astype(_BF16)
        a_ref[r:r + half, :] = (_dot(h, win_ref[:, :e]) * scale_a).astype(_BF16)
        b_ref[r:r + half, :] = _dot(h, win_ref[:, e:]).astype(_BF16)


def _out_next(x, y, w_out, g, w_in, scale_a):
    m, d = x.shape
    e = y.shape[1]
    tm = PROJ_TILE
    half_out = jax.ShapeDtypeStruct((m, e), _BF16)
    return pl.pallas_call(
        functools.partial(_out_next_kernel, e=e, scale_a=scale_a),
        grid=(m // tm,),
        in_specs=[pl.BlockSpec((tm, d), lambda i: (i, 0)),
                  pl.BlockSpec((tm, e), lambda i: (i, 0)),
                  _const_spec((e, d)),
                  _const_spec((1, d)),
                  _const_spec((d, 2 * e))],
        out_specs=[pl.BlockSpec((tm, d), lambda i: (i, 0)),
                   pl.BlockSpec((tm, e), lambda i: (i, 0)),
                   pl.BlockSpec((tm, e), lambda i: (i, 0))],
        out_shape=[jax.ShapeDtypeStruct((m, d), _F32), half_out, half_out],
        compiler_params=_params(1),
        name="out_next",
    )(x, y, w_out, g, w_in)


def kernel(x, a_norm, a_w_in, a_ln_g, a_ln_b, a_w_s, a_b_s, a_w_out, kv_norm, w_kv,
           b_norm, b_w_in, b_w_out, final_norm):
    batch, s_len, d = x.shape
    e = w_kv.shape[1] // 2
    n_a, n_b = a_w_in.shape[0], b_w_in.shape[0]
    assert n_b >= 1 and a_w_s.shape[1:] == (GMLP_GROUPS, GMLP_BLOCK, GMLP_BLOCK)
    assert (batch * s_len) % PROJ_TILE == 0 and s_len % ATT_BLOCK == 0 and s_len >= 2 * ATT_BLOCK
    assert GMLP_TILE % GMLP_BLOCK == 0 and SB_HEADS % ATT_HEADS_PER_STEP == 0

    xs = x.reshape(batch * s_len, d)
    for i in range(n_a):
        xs = _gmlp_layer(xs, a_norm[i][None], a_w_in[i].astype(_BF16), a_ln_g[i][None],
                         a_ln_b[i][None], a_w_s[i], a_b_s[i].T, a_w_out[i].astype(_BF16))

    k, v = _norm_proj(xs, kv_norm[None], w_kv.astype(_BF16))
    scale = 1.0 / math.sqrt(e // SB_HEADS)
    q, z = _norm_proj(xs, b_norm[0][None], b_w_in[0].astype(_BF16), scale_a=scale)
    for i in range(n_b):
        y = _stick_breaking(q, k, v, z, batch, SB_HEADS)
        w_out = b_w_out[i].astype(_BF16)
        if i + 1 < n_b:
            xs, q, z = _out_next(xs, y, w_out, b_norm[i + 1][None],
                                 b_w_in[i + 1].astype(_BF16), scale)
        else:
            xs = _out_final(xs, y, w_out, final_norm[None])
    return xs.reshape(batch, s_len, d)
```

```python
import functools
import math

import jax
import jax.numpy as jnp
from jax import lax
from jax.experimental import pallas as pl
from jax.experimental.pallas import tpu as pltpu

EPS = 1e-6
CHUNK = 64
GMLP_BLOCK = 128
GMLP_GROUPS = 8
SB_HEADS = 16

VMEM_LIMIT_BYTES_V7X = 56 * 1024 * 1024

GMLP_TILE = 512
PROJ_TILE = 512
ATT_BLOCK = 256
ATT_HEADS_PER_STEP = 4
ATT_BLOCKS_PER_ITER = 5

ATT_DEAD_LOG = -105.0
MASKED_LOGIT = -1e30
LOG2_E = math.log2(math.e)

_BF16 = jnp.bfloat16
_F32 = jnp.float32


def _dot(a, b):
    return jnp.dot(a, b, preferred_element_type=_F32)


def _wdot(a, w):
    return _dot(a, w.astype(_BF16))


def _gelu(x):
    c = math.sqrt(2.0 / math.pi)
    hx = 0.5 * x
    return hx + hx * jnp.tanh(x * (c + (c * 0.044715) * (x * x)))


def _silu(z):
    hz = 0.5 * z
    return hz + hz * jnp.tanh(hz)


def _rmsnorm(x, g):
    return x * lax.rsqrt(jnp.mean(x * x, axis=-1, keepdims=True) + EPS) * g


def _const_spec(shape, layer=None):
    nd = len(shape)
    if layer is None:
        return pl.BlockSpec(shape, lambda *_: (0,) * nd, pipeline_mode=pl.Buffered(1))
    return pl.BlockSpec((None,) + tuple(shape), lambda *_: (layer,) + (0,) * nd,
                        pipeline_mode=pl.Buffered(1))


def _params(n_axes):
    return pltpu.CompilerParams(
        dimension_semantics=("arbitrary",) * n_axes,
        vmem_limit_bytes=VMEM_LIMIT_BYTES_V7X)


def _gmlp_kernel(x_ref, g_ref, win_ref, lng_ref, lnb_ref, ws_ref, bs_ref, wout_ref,
                 o_ref, vn_ref, y_ref, *, e, groups, blk):
    tm = x_ref.shape[0]
    gd = e // groups
    x = x_ref[...]
    h = _rmsnorm(x, g_ref[...]).astype(_BF16)

    v = _gelu(_wdot(h, win_ref[:, e:2 * e]))
    mu = jnp.mean(v, axis=-1, keepdims=True)
    vc = v - mu
    var = jnp.mean(vc * vc, axis=-1, keepdims=True)
    vn = vc * lax.rsqrt(var + EPS) * lng_ref[...] + lnb_ref[...]
    vn_ref[...] = vn.astype(_BF16)

    r = lax.broadcasted_iota(jnp.int32, (blk, blk), 0) // CHUNK
    c = lax.broadcasted_iota(jnp.int32, (blk, blk), 1) // CHUNK
    mask = (r >= c).astype(_F32)

    for g in range(groups):
        lo, hi = g * gd, (g + 1) * gd
        u = _gelu(_wdot(h, win_ref[:, lo:hi]))
        z = _wdot(h, win_ref[:, 2 * e + lo:2 * e + hi])
        wm = (ws_ref[g] * mask).astype(_BF16)
        bcol = bs_ref[:, g:g + 1]
        s = jnp.concatenate(
            [_dot(wm, vn_ref[b * blk:(b + 1) * blk, lo:hi]) + bcol
             for b in range(tm // blk)], axis=0)
        y_ref[:, lo:hi] = (u * s * _silu(z)).astype(_BF16)

    o_ref[...] = x + _wdot(y_ref[...], wout_ref[...])


def _gmlp_layer(x, g, w_in, ln_g, ln_b, w_s, b_s_t, w_out, layer):
    m, d = x.shape
    e = w_out.shape[1]
    groups, blk = w_s.shape[1], w_s.shape[2]
    tm = GMLP_TILE
    kern = functools.partial(_gmlp_kernel, e=e, groups=groups, blk=blk)
    return pl.pallas_call(
        kern,
        grid=(m // tm,),
        in_specs=[
            pl.BlockSpec((tm, d), lambda i: (i, 0)),
            _const_spec((1, d)),
            _const_spec((d, 3 * e), layer),
            _const_spec((1, e)),
            _const_spec((1, e)),
            _const_spec((groups, blk, blk), layer),
            _const_spec((blk, groups)),
            _const_spec((e, d), layer),
        ],
        out_specs=pl.BlockSpec((tm, d), lambda i: (i, 0)),
        out_shape=jax.ShapeDtypeStruct((m, d), _F32),
        scratch_shapes=[pltpu.VMEM((tm, e), _BF16), pltpu.VMEM((tm, e), _BF16)],
        compiler_params=_params(1),
        name="gmlp_layer",
    )(x, g, w_in, ln_g, ln_b, w_s, b_s_t, w_out)


def _proj_kernel(x_ref, g_ref, w_ref, a_ref, b_ref, *, e, scale_a):
    h = _rmsnorm(x_ref[...], g_ref[...]).astype(_BF16)
    a = _wdot(h, w_ref[:, :e])
    if scale_a != 1.0:
        a = a * scale_a
    a_ref[...] = a.astype(_BF16)
    b_ref[...] = _wdot(h, w_ref[:, e:]).astype(_BF16)


def _norm_proj(x, g, w, scale_a=1.0, layer=None):
    m, d = x.shape
    e = w.shape[-1] // 2
    tm = PROJ_TILE
    kern = functools.partial(_proj_kernel, e=e, scale_a=scale_a)
    out = jax.ShapeDtypeStruct((m, e), _BF16)
    return pl.pallas_call(
        kern,
        grid=(m // tm,),
        in_specs=[
            pl.BlockSpec((tm, d), lambda i: (i, 0)),
            _const_spec((1, d)),
            _const_spec((d, 2 * e), layer),
        ],
        out_specs=[pl.BlockSpec((tm, e), lambda i: (i, 0))] * 2,
        out_shape=[out, out],
        compiler_params=_params(1),
        name="norm_proj",
    )(x, g, w)


def _att_logits(q, kw):
    return lax.dot_general(q, kw, (((1,), (1,)), ((), ())),
                           preferred_element_type=_F32)


def _att_weights(logit, u, carry):
    sp = jnp.log(1.0 + jnp.exp2(jnp.abs(logit) * (-LOG2_E)))
    log_beta = jnp.minimum(logit, 0.0) - sp
    log_1m = log_beta - logit
    after = _dot(log_1m.astype(_BF16), u)
    if carry is not None:
        after = after + carry
    att = jnp.exp(log_beta + after).astype(_BF16)
    return att, after[:, 0:1] + log_1m[:, 0:1]


def _att_kernel(q_ref, k_ref, v_ref, z_ref, y_ref, u_ref, ud_ref, bias_ref, carry_ref, acc_ref,
                worst_ref, *head_refs, blk, dh):
    s_len = q_ref.shape[0]
    n_blk = s_len // blk
    unit = blk // 2
    heads = range(q_ref.shape[1] // dh)
    lgd_ref, lgp_ref, attd_ref, attp_ref = (head_refs[n::4] for n in range(4))

    def suffix_sum_matrix(keys):
        row = lax.broadcasted_iota(jnp.int32, (keys, keys), 0)
        col = lax.broadcasted_iota(jnp.int32, (keys, keys), 1)
        return (row > col).astype(_BF16)

    u_ref[...] = suffix_sum_matrix(blk)
    ud_ref[...] = suffix_sum_matrix(unit)
    row = lax.broadcasted_iota(jnp.int32, (unit, unit), 0)
    col = lax.broadcasted_iota(jnp.int32, (unit, unit), 1)
    bias_ref[...] = jnp.where(col < row, 0.0, MASKED_LOGIT)

    def tile(ref, start, j, rows):
        return ref[pl.ds(start, rows), j * dh:(j + 1) * dh]

    def unit_rows(t0, n):
        return pl.multiple_of(t0 + n * unit, unit)

    def prev_start(r0):
        return pl.multiple_of(jnp.maximum(r0 - blk, 0), unit)

    def store_logits(t0, j):
        for n in range(2):
            r0 = unit_rows(t0, n)
            q = tile(q_ref, r0, j, unit)
            lgd_ref[j][n] = _att_logits(q, tile(k_ref, r0, j, unit))
            lgp_ref[j][n] = _att_logits(q, tile(k_ref, prev_start(r0), j, blk))

    def store_weights(j):
        carries = []
        for n in range(2):
            att, carry = _att_weights(lgd_ref[j][n] + bias_ref[...], ud_ref[...], None)
            attd_ref[j][n] = att
            attp_ref[j][n], carry = _att_weights(lgp_ref[j][n], u_ref[...], carry)
            carries.append(carry)
        return jnp.concatenate(carries, axis=0)

    def store_first_weights(j):
        carries = []
        for n in range(2):
            attd_ref[j][n], carry = _att_weights(lgd_ref[j][n] + bias_ref[...], ud_ref[...], None)
            carries.append(carry)
        att, _ = _att_weights(lgp_ref[j][1, :, :unit], ud_ref[...], carries[1])
        attp_ref[j][0] = jnp.zeros((unit, blk), _BF16)
        attp_ref[j][1] = jnp.concatenate([att, jnp.zeros((unit, unit), _BF16)], axis=1)

    def finish_block(t0, j):
        accs = []
        for n in range(2):
            r0 = unit_rows(t0, n)
            accs.append(_dot(attd_ref[j][n], tile(v_ref, r0, j, unit))
                        + _dot(attp_ref[j][n], tile(v_ref, prev_start(r0), j, blk)))
        z = tile(z_ref, t0, j, blk).astype(_F32)
        y_ref[pl.ds(t0, blk), j * dh:(j + 1) * dh] = (
            jnp.concatenate(accs, axis=0) * _silu(z)).astype(_BF16)

    def step(i):
        t0 = pl.multiple_of(i * blk, blk)
        tp = pl.multiple_of(t0 - blk, blk)
        tn = pl.multiple_of(jnp.minimum(t0 + blk, s_len - blk), blk)
        carries = []
        for j in heads:
            finish_block(tp, j)
            carries.append(store_weights(j))
            store_logits(tn, j)
        worst_ref[i] = jnp.max(functools.reduce(jnp.maximum, carries))

    def redo_unit(r0):
        worst = []
        for j in heads:
            logit = _att_logits(tile(q_ref, r0, j, unit), tile(k_ref, r0, j, unit))
            att, carry = _att_weights(logit + bias_ref[...], ud_ref[...], None)
            acc_ref[j] = _dot(att, tile(v_ref, r0, j, unit))
            carry_ref[j] = carry
            worst.append(jnp.max(carry))

        def live(st):
            start, w = st
            return jnp.logical_and(start > 0, w > ATT_DEAD_LOG)

        def window(st):
            start = pl.multiple_of(st[0] - unit, unit)
            worst = []
            for j in heads:
                logit = _att_logits(tile(q_ref, r0, j, unit), tile(k_ref, start, j, unit))
                att, carry = _att_weights(logit, ud_ref[...], carry_ref[j])
                acc_ref[j] += _dot(att, tile(v_ref, start, j, unit))
                carry_ref[j] = carry
                worst.append(jnp.max(carry))
            return start, functools.reduce(jnp.maximum, worst)

        lax.while_loop(live, window, (r0, functools.reduce(jnp.maximum, worst)))
        for j in heads:
            z = tile(z_ref, r0, j, unit).astype(_F32)
            y_ref[pl.ds(r0, unit), j * dh:(j + 1) * dh] = (acc_ref[j] * _silu(z)).astype(_BF16)

    for j in heads:
        store_logits(0, j)
    for j in heads:
        store_first_weights(j)
    for j in heads:
        store_logits(blk, j)

    per = ATT_BLOCKS_PER_ITER

    def steps(p, c):
        for n in range(per):
            step(per * p + 1 + n)
        return c

    n_iter = (n_blk - 1) // per
    lax.fori_loop(0, n_iter, steps, 0)
    for i in range(per * n_iter + 1, n_blk):
        step(jnp.int32(i))
    for j in heads:
        finish_block((n_blk - 1) * blk, j)

    def redo_if_live(i, c):
        @pl.when(worst_ref[i] > ATT_DEAD_LOG)
        def _():
            redo_unit(pl.multiple_of(i * blk, blk))
            redo_unit(pl.multiple_of(i * blk + unit, unit))
        return c

    lax.fori_loop(1, n_blk, redo_if_live, 0)


def _stick_breaking(q, k, v, z, batch, heads):
    m, e = q.shape
    s_len = m // batch
    dh = e // heads
    blk, nh = ATT_BLOCK, ATT_HEADS_PER_STEP
    unit = blk // 2
    spec = pl.BlockSpec((s_len, nh * dh), lambda b, h: (b, h))
    per_head = [pltpu.VMEM((2, unit, unit), _F32),
                pltpu.VMEM((2, unit, blk), _F32),
                pltpu.VMEM((2, unit, unit), _BF16),
                pltpu.VMEM((2, unit, blk), _BF16)]
    return pl.pallas_call(
        functools.partial(_att_kernel, blk=blk, dh=dh),
        grid=(batch, heads // nh),
        in_specs=[spec] * 4,
        out_specs=spec,
        out_shape=jax.ShapeDtypeStruct((m, e), _BF16),
        scratch_shapes=[pltpu.VMEM((blk, blk), _BF16),
                        pltpu.VMEM((unit, unit), _BF16),
                        pltpu.VMEM((unit, unit), _F32),
                        pltpu.VMEM((nh, unit, 1), _F32),
                        pltpu.VMEM((nh, unit, dh), _F32),
                        pltpu.SMEM((s_len // blk,), _F32)] + per_head * nh,
        compiler_params=_params(2),
        name="stick_breaking",
    )(q, k, v, z)


def _out_final_kernel(x_ref, y_ref, w_ref, gf_ref, o_ref):
    o_ref[...] = _rmsnorm(x_ref[...] + _wdot(y_ref[...], w_ref[...]), gf_ref[...])


def _out_final(x, y, w, final_g, layer):
    m, d = x.shape
    e = y.shape[1]
    tm = PROJ_TILE
    return pl.pallas_call(
        _out_final_kernel,
        grid=(m // tm,),
        in_specs=[pl.BlockSpec((tm, d), lambda i: (i, 0)),
                  pl.BlockSpec((tm, e), lambda i: (i, 0)),
                  _const_spec((e, d), layer),
                  _const_spec((1, d))],
        out_specs=pl.BlockSpec((tm, d), lambda i: (i, 0)),
        out_shape=jax.ShapeDtypeStruct((m, d), _F32),
        compiler_params=_params(1),
        name="out_final",
    )(x, y, w, final_g)


def _out_next_kernel(x_ref, y_ref, wout_ref, g_ref, win_ref, o_ref, a_ref, b_ref, *, e, scale_a):
    half = x_ref.shape[0] // 2
    new = [x_ref[r:r + half, :] + _wdot(y_ref[r:r + half, :], wout_ref[...])
           for r in (0, half)]
    for n, r in enumerate((0, half)):
        o_ref[r:r + half, :] = new[n]
        h = _rmsnorm(new[n], g_ref[...]).astype(_BF16)
        a_ref[r:r + half, :] = (_wdot(h, win_ref[:, :e]) * scale_a).astype(_BF16)
        b_ref[r:r + half, :] = _wdot(h, win_ref[:, e:]).astype(_BF16)


def _out_next(x, y, w_out, g, w_in, scale_a, layer):
    m, d = x.shape
    e = y.shape[1]
    tm = PROJ_TILE
    half_out = jax.ShapeDtypeStruct((m, e), _BF16)
    return pl.pallas_call(
        functools.partial(_out_next_kernel, e=e, scale_a=scale_a),
        grid=(m // tm,),
        in_specs=[pl.BlockSpec((tm, d), lambda i: (i, 0)),
                  pl.BlockSpec((tm, e), lambda i: (i, 0)),
                  _const_spec((e, d), layer),
                  _const_spec((1, d)),
                  _const_spec((d, 2 * e), layer + 1)],
        out_specs=[pl.BlockSpec((tm, d), lambda i: (i, 0)),
                   pl.BlockSpec((tm, e), lambda i: (i, 0)),
                   pl.BlockSpec((tm, e), lambda i: (i, 0))],
        out_shape=[jax.ShapeDtypeStruct((m, d), _F32), half_out, half_out],
        compiler_params=_params(1),
        name="out_next",
    )(x, y, w_out, g, w_in)


def kernel(x, a_norm, a_w_in, a_ln_g, a_ln_b, a_w_s, a_b_s, a_w_out, kv_norm, w_kv,
           b_norm, b_w_in, b_w_out, final_norm):
    batch, s_len, d = x.shape
    e = w_kv.shape[1] // 2
    n_a, n_b = a_w_in.shape[0], b_w_in.shape[0]
    assert n_b >= 1 and a_w_s.shape[1:] == (GMLP_GROUPS, GMLP_BLOCK, GMLP_BLOCK)
    assert (batch * s_len) % PROJ_TILE == 0 and s_len % ATT_BLOCK == 0 and s_len >= 2 * ATT_BLOCK
    assert GMLP_TILE % GMLP_BLOCK == 0 and SB_HEADS % ATT_HEADS_PER_STEP == 0

    xs = x.reshape(batch * s_len, d)
    for i in range(n_a):
        xs = _gmlp_layer(xs, a_norm[i][None], a_w_in, a_ln_g[i][None],
                         a_ln_b[i][None], a_w_s, a_b_s[i].T, a_w_out, i)

    k, v = _norm_proj(xs, kv_norm[None], w_kv)
    scale = 1.0 / math.sqrt(e // SB_HEADS)
    q, z = _norm_proj(xs, b_norm[0][None], b_w_in, scale_a=scale, layer=0)
    for i in range(n_b):
        y = _stick_breaking(q, k, v, z, batch, SB_HEADS)
        if i + 1 < n_b:
            xs, q, z = _out_next(xs, y, b_w_out, b_norm[i + 1][None], b_w_in, scale, i)
        else:
            xs = _out_final(xs, y, b_w_out, final_norm[None], i)
    return xs.reshape(batch, s_len, d)
```
